```python
import jax, jax.numpy as jnp
from jax import lax
import numpy as np

D_MODEL = 2048
BATCH = 2
SEQ = 4096
DEPTH = 4
DEC_BATCH = 32
DEC_SEQ = 4
PAST_LEN = 16384
PAGE_SIZE = 128

HEAD_DIM = 128
MIX_WIDTH = D_MODEL
N_HEADS_A = MIX_WIDTH // (2 * HEAD_DIM)
KV_HEADS_A = max(1, N_HEADS_A // 4)
GROUP_A = N_HEADS_A // KV_HEADS_A
N_HEADS_B = MIX_WIDTH // HEAD_DIM - N_HEADS_A
WIDTH_A = N_HEADS_A * HEAD_DIM
KV_WIDTH_A = KV_HEADS_A * HEAD_DIM
WIDTH_B = N_HEADS_B * HEAD_DIM
SPLIT_POINTS = (WIDTH_A, WIDTH_A + KV_WIDTH_A, WIDTH_A + 2 * KV_WIDTH_A,
                WIDTH_A + 2 * KV_WIDTH_A + WIDTH_B, WIDTH_A + 2 * KV_WIDTH_A + 2 * WIDTH_B)
PROJ_WIDTH = WIDTH_A + 2 * KV_WIDTH_A + 3 * WIDTH_B
WINDOW_A = 128
DILATED_BRANCHES = ((128, 1), (512, 4), (2048, 16))
WINDOW_B_MAX = 2048
BLOCK = 128
D_FF_DENSE = 5632
N_EXPERTS = 8
TOP_K = 2
D_FF_EXPERT = 7168
N_DENSE = (DEPTH + 1) // 2
N_MOE = DEPTH // 2
EPS = 1e-5
NEG_INF = -1e30

kernel_name = 'hymba_swa_sink_dilated_moe_step'


def rmsnorm(x, g):
    xf = x.astype(jnp.float32)
    y = xf * lax.rsqrt(jnp.mean(xf * xf, axis=-1, keepdims=True) + EPS)
    return (y * g.astype(jnp.float32)).astype(x.dtype)


def alibi_slopes(n):
    return 2.0 ** (-8.0 * jnp.arange(1, n + 1, dtype=jnp.float32) / n)


def attn_core(q, k, v, dist, valid, slopes, sink=None):
    d = q.shape[-1]
    s = jnp.einsum('bnqhgd,bnkhd->bnhgqk', q, k, preferred_element_type=jnp.float32) * (d ** -0.5)
    s = s - slopes[:, :, None, None] * dist[:, :, None, None].astype(jnp.float32)
    s = jnp.where(valid[:, :, None, None], s, NEG_INF)
    m = jnp.max(s, axis=-1)
    if sink is not None:
        sink = sink.astype(jnp.float32)[:, :, None]
        m = jnp.maximum(m, sink)
    p = jnp.exp(s - m[..., None])
    l = jnp.sum(p, axis=-1)
    if sink is not None:
        l = l + jnp.exp(sink - m)
    o = jnp.einsum('bnhgqk,bnkhd->bnqhgd', p / l[..., None], v.astype(jnp.float32))
    lse = jnp.moveaxis(m + jnp.log(l), -1, 2)
    return o, lse


def band_attention(q, k, v, win, slopes, sink=None):
    b, length, hk, g, d = q.shape
    nb = length // BLOCK

    def prev_cur(t):
        cur = t.reshape(b, nb, BLOCK, hk, d)
        prev = jnp.pad(cur, ((0, 0), (1, 0), (0, 0), (0, 0), (0, 0)))[:, :nb]
        return jnp.concatenate([prev, cur], axis=2)

    i = jnp.arange(BLOCK)[:, None]
    j = jnp.arange(2 * BLOCK)[None, :]
    dist = BLOCK + i - j
    has_prev = (jnp.arange(nb)[:, None, None] > 0) | (j >= BLOCK)[None]
    valid = ((dist >= 0) & (dist <= win))[None] & has_prev
    o, lse = attn_core(q.reshape(b, nb, BLOCK, hk, g, d), prev_cur(k), prev_cur(v),
                       dist[None, None], valid[None], slopes, sink)
    return o.reshape(b, length, hk, g, d), lse.reshape(b, length, hk, g)


def combine_branches(outs, lses):
    wts = jax.nn.softmax(jnp.stack(lses), axis=0)
    return jnp.sum(wts[..., None] * jnp.stack(outs), axis=0)


def dilated_prompt(q, k, v, slopes):
    b, s, h, d = q.shape
    outs, lses = [], []
    for w, r in DILATED_BRANCHES:
        length = s // r
        padded = -(-length // BLOCK) * BLOCK

        def fold(t):
            t = t.reshape(b, length, r, h, d).transpose(0, 2, 1, 3, 4).reshape(b * r, length, h, d)
            return jnp.pad(t, ((0, 0), (0, padded - length), (0, 0), (0, 0)))

        o, lse = band_attention(fold(q)[:, :, :, None], fold(k), fold(v), w // r, (slopes * r)[:, None])
        outs.append(o[:, :length, :, 0].reshape(b, r, length, h, d).transpose(0, 2, 1, 3, 4).reshape(b, s, h, d))
        lses.append(lse[:, :length, :, 0].reshape(b, r, length, h).transpose(0, 2, 1, 3).reshape(b, s, h))
    return combine_branches(outs, lses)


def dilated_sample(q, k_all, v_all, rows, slopes):
    t = q.shape[1]
    outs, lses = [], []
    for w, r in DILATED_BRANCHES:
        j = np.arange(w // r + 1)
        idx = rows + np.arange(t)[:, None] - r * j[None, :]
        gidx = jnp.asarray(np.maximum(idx, 0), jnp.int32)
        valid = jnp.asarray(idx >= 0)[None, :, None, :]
        dist = jnp.asarray(r * j, jnp.float32)[None, None, None, :]
        o, lse = attn_core(q[:, :, None, :, None, :], jnp.take(k_all, gidx, axis=1),
                           jnp.take(v_all, gidx, axis=1), dist, valid, slopes[:, None])
        outs.append(o[:, :, 0, :, 0])
        lses.append(lse[:, :, 0, :, 0])
    return combine_branches(outs, lses)


def window_sample(q, k_all, v_all, rows, slopes, sink):
    t = q.shape[1]
    kpos = np.concatenate([np.arange(-rows, 0), np.arange(t)])
    dist = np.arange(t)[:, None] - kpos[None, :]
    valid = (dist >= 0) & (dist <= WINDOW_A)
    o, _ = attn_core(q[:, None], k_all[:, None], v_all[:, None],
                     jnp.asarray(dist, jnp.float32)[None, None], jnp.asarray(valid)[None, None], slopes, sink)
    return o[:, 0]


def project(h, w):
    qa, ka, va, qb, kb, vb = jnp.split(h @ w, list(SPLIT_POINTS), axis=-1)
    lead = h.shape[:-1]
    return (qa.reshape(*lead, KV_HEADS_A, GROUP_A, HEAD_DIM),
            ka.reshape(*lead, KV_HEADS_A, HEAD_DIM),
            va.reshape(*lead, KV_HEADS_A, HEAD_DIM),
            qb.reshape(*lead, N_HEADS_B, HEAD_DIM),
            kb.reshape(*lead, N_HEADS_B, HEAD_DIM),
            vb.reshape(*lead, N_HEADS_B, HEAD_DIM))


def merge_heads(oa, ob, g_a, g_b, w):
    lead = oa.shape[:2]
    o = jnp.concatenate([rmsnorm(oa.reshape(*lead, WIDTH_A).astype(w.dtype), g_a),
                         rmsnorm(ob.reshape(*lead, WIDTH_B).astype(w.dtype), g_b)], axis=-1)
    return o @ w


def swiglu(h, wg, wu, wd):
    return (jax.nn.silu(h @ wg) * (h @ wu)) @ wd


def moe_ffn(h, w_router, w_gate, w_up, w_down):
    logits = jnp.einsum('...d,de->...e', h, w_router, preferred_element_type=jnp.float32)
    top_v, top_i = lax.top_k(logits, TOP_K)
    gates = jax.nn.softmax(top_v, axis=-1)
    comb = jnp.sum(jax.nn.one_hot(top_i, N_EXPERTS, dtype=jnp.float32) * gates[..., None], axis=-2)
    y = jnp.zeros(h.shape, jnp.float32)
    for e in range(N_EXPERTS):
        y = y + comb[..., e:e + 1] * swiglu(h, w_gate[e], w_up[e], w_down[e]).astype(jnp.float32)
    return y.astype(h.dtype)


def channel_mixer(h, l, w_gate_dense, w_up_dense, w_down_dense, w_router, w_gate_moe, w_up_moe, w_down_moe):
    i = l // 2
    if l % 2 == 0:
        return swiglu(h, w_gate_dense[i], w_up_dense[i], w_down_dense[i])
    return moe_ffn(h, w_router[i], w_gate_moe[i], w_up_moe[i], w_down_moe[i])


def setup_inputs(seed: int = 0) -> dict:
    key = jax.random.key(seed)
    ks = jax.random.split(key, 22)
    f32 = jnp.float32

    def nrm(k, shape, scale=1.0):
        return jax.random.normal(k, shape, f32) * scale

    def gain(k, shape):
        return 1.0 + 0.02 * jax.random.normal(k, shape, f32)

    rows_a = min(WINDOW_A, PAST_LEN)
    rows_b = min(WINDOW_B_MAX, PAST_LEN)
    return {
        'x_prompt': nrm(ks[0], (BATCH, SEQ, D_MODEL)),
        'x_sample': nrm(ks[1], (DEC_BATCH, DEC_SEQ, D_MODEL)),
        'cache_a_k': nrm(ks[2], (DEPTH, DEC_BATCH, rows_a, KV_HEADS_A, HEAD_DIM)),
        'cache_a_v': nrm(ks[3], (DEPTH, DEC_BATCH, rows_a, KV_HEADS_A, HEAD_DIM)),
        'cache_b_k': nrm(ks[4], (DEPTH, DEC_BATCH, rows_b, N_HEADS_B, HEAD_DIM)),
        'cache_b_v': nrm(ks[5], (DEPTH, DEC_BATCH, rows_b, N_HEADS_B, HEAD_DIM)),
        'g_mix': gain(ks[6], (DEPTH, D_MODEL)),
        'w_in': nrm(ks[7], (DEPTH, D_MODEL, PROJ_WIDTH), D_MODEL ** -0.5),
        'sink_a': nrm(ks[8], (DEPTH, N_HEADS_A), 0.5),
        'g_out_a': gain(ks[9], (DEPTH, WIDTH_A)),
        'g_out_b': gain(ks[10], (DEPTH, WIDTH_B)),
        'w_out': nrm(ks[11], (DEPTH, MIX_WIDTH, D_MODEL), MIX_WIDTH ** -0.5),
        'g_ffn': gain(ks[12], (DEPTH, D_MODEL)),
        'w_gate_dense': nrm(ks[13], (N_DENSE, D_MODEL, D_FF_DENSE), D_MODEL ** -0.5),
        'w_up_dense': nrm(ks[14], (N_DENSE, D_MODEL, D_FF_DENSE), D_MODEL ** -0.5),
        'w_down_dense': nrm(ks[15], (N_DENSE, D_FF_DENSE, D_MODEL), D_FF_DENSE ** -0.5),
        'w_router': nrm(ks[16], (N_MOE, D_MODEL, N_EXPERTS), D_MODEL ** -0.5),
        'w_gate_moe': nrm(ks[17], (N_MOE, N_EXPERTS, D_MODEL, D_FF_EXPERT), D_MODEL ** -0.5),
        'w_up_moe': nrm(ks[18], (N_MOE, N_EXPERTS, D_MODEL, D_FF_EXPERT), D_MODEL ** -0.5),
        'w_down_moe': nrm(ks[19], (N_MOE, N_EXPERTS, D_FF_EXPERT, D_MODEL), D_FF_EXPERT ** -0.5),
        'g_final': gain(ks[20], (D_MODEL,)),
    }


def reference(x_prompt, x_sample, cache_a_k, cache_a_v, cache_b_k, cache_b_v,
              g_mix, w_in, sink_a, g_out_a, g_out_b, w_out, g_ffn,
              w_gate_dense, w_up_dense, w_down_dense,
              w_router, w_gate_moe, w_up_moe, w_down_moe, g_final):
    slopes = alibi_slopes(N_HEADS_A + N_HEADS_B)
    slopes_a = slopes[:N_HEADS_A].reshape(KV_HEADS_A, GROUP_A)
    slopes_b = slopes[N_HEADS_A:]
    xp, xs = x_prompt, x_sample
    seq = xp.shape[1]
    keep_a, keep_b = min(WINDOW_A, seq), min(WINDOW_B_MAX, seq)
    rows_a, rows_b = cache_a_k.shape[2], cache_b_k.shape[2]
    nak_p, nav_p, nbk_p, nbv_p = [], [], [], []
    nak_s, nav_s, nbk_s, nbv_s = [], [], [], []
    for l in range(DEPTH):
        sink_l = sink_a[l].reshape(KV_HEADS_A, GROUP_A)
        qa, ka, va, qb, kb, vb = project(rmsnorm(xp, g_mix[l]), w_in[l])
        oa, _ = band_attention(qa, ka, va, WINDOW_A, slopes_a, sink_l)
        ob = dilated_prompt(qb, kb, vb, slopes_b)
        xp = xp + merge_heads(oa, ob, g_out_a[l], g_out_b[l], w_out[l])
        xp = xp + channel_mixer(rmsnorm(xp, g_ffn[l]), l, w_gate_dense, w_up_dense, w_down_dense,
                                w_router, w_gate_moe, w_up_moe, w_down_moe)
        nak_p.append(ka[:, seq - keep_a:])
        nav_p.append(va[:, seq - keep_a:])
        nbk_p.append(kb[:, seq - keep_b:])
        nbv_p.append(vb[:, seq - keep_b:])
        qa, ka, va, qb, kb, vb = project(rmsnorm(xs, g_mix[l]), w_in[l])
        ka_all = jnp.concatenate([cache_a_k[l].astype(ka.dtype), ka], axis=1)
        va_all = jnp.concatenate([cache_a_v[l].astype(va.dtype), va], axis=1)
        kb_all = jnp.concatenate([cache_b_k[l].astype(kb.dtype), kb], axis=1)
        vb_all = jnp.concatenate([cache_b_v[l].astype(vb.dtype), vb], axis=1)
        oa = window_sample(qa, ka_all, va_all, rows_a, slopes_a, sink_l)
        ob = dilated_sample(qb, kb_all, vb_all, rows_b, slopes_b)
        xs = xs + merge_heads(oa, ob, g_out_a[l], g_out_b[l], w_out[l])
        xs = xs + channel_mixer(rmsnorm(xs, g_ffn[l]), l, w_gate_dense, w_up_dense, w_down_dense,
                                w_router, w_gate_moe, w_up_moe, w_down_moe)
        nak_s.append(ka_all[:, ka_all.shape[1] - rows_a:])
        nav_s.append(va_all[:, va_all.shape[1] - rows_a:])
        nbk_s.append(kb_all[:, kb_all.shape[1] - rows_b:])
        nbv_s.append(vb_all[:, vb_all.shape[1] - rows_b:])
    y_prompt = rmsnorm(xp, g_final)
    y_sample = rmsnorm(xs, g_final)
    return (y_prompt, y_sample,
            jnp.stack(nak_p), jnp.stack(nav_p), jnp.stack(nbk_p), jnp.stack(nbv_p),
            jnp.stack(nak_s), jnp.stack(nav_s), jnp.stack(nbk_s), jnp.stack(nbv_s))
```

```python
import functools

import jax
import jax.numpy as jnp
import numpy as np
from jax import lax
from jax.experimental import pallas as pl
from jax.experimental.pallas import tpu as pltpu

F32 = jnp.float32
BF16 = jnp.bfloat16

D_MODEL = 2048
HEAD_DIM = 128
N_HEADS_A = 8
KV_HEADS_A = 2
GROUP_A = 4
N_HEADS_B = 8
WIDTH_A = N_HEADS_A * HEAD_DIM
KV_WIDTH_A = KV_HEADS_A * HEAD_DIM
WIDTH_B = N_HEADS_B * HEAD_DIM
PROJ_WIDTH = WIDTH_A + 2 * KV_WIDTH_A + 3 * WIDTH_B
COL_KA = WIDTH_A
COL_VA = COL_KA + KV_WIDTH_A
COL_QB = COL_VA + KV_WIDTH_A
COL_KB = COL_QB + WIDTH_B
COL_VB = COL_KB + WIDTH_B
WINDOW_A = 128
DILATED_BRANCHES = ((128, 1), (512, 4), (2048, 16))
WINDOW_B_MAX = 2048
BLOCK = 128
N_EXPERTS = 8
TOP_K = 2
EPS = 1e-5
NEG_INF = -1e30
SCALE = HEAD_DIM ** -0.5
HEADS_PER_STEP = 4
SLOPES = tuple(2.0 ** (-8.0 * i / (N_HEADS_A + N_HEADS_B)) for i in range(1, N_HEADS_A + N_HEADS_B + 1))
SLOPES_A = SLOPES[:N_HEADS_A]
SLOPES_B = SLOPES[N_HEADS_A:]

MIB = 1024 * 1024


def _params(semantics, vmem_mib):
    return pltpu.CompilerParams(dimension_semantics=semantics, vmem_limit_bytes=vmem_mib * MIB)


def _bdot(a, b):
    return jnp.dot(a.astype(BF16), b.astype(BF16), preferred_element_type=F32)


def _bdot_nt(a, b):
    return lax.dot_general(a.astype(BF16), b.astype(BF16), (((1,), (1,)), ((), ())),
                           preferred_element_type=F32)


def _rms(x, g):
    return x * lax.rsqrt(jnp.mean(x * x, axis=-1, keepdims=True) + EPS) * g


def _rmsnorm_kernel(x_ref, g_ref, o_ref):
    o_ref[...] = _rms(x_ref[...], g_ref[...]).astype(o_ref.dtype)


def rmsnorm_rows(x, g, out_dtype, row_block, first_block, n_blocks):
    d = x.shape[1]
    return pl.pallas_call(
        _rmsnorm_kernel,
        grid=(n_blocks,),
        in_specs=[pl.BlockSpec((row_block, d), lambda i: (i + first_block, 0)),
                  pl.BlockSpec((1, d), lambda i: (0, 0))],
        out_specs=pl.BlockSpec((row_block, d), lambda i: (i, 0)),
        out_shape=jax.ShapeDtypeStruct((n_blocks * row_block, d), out_dtype),
        compiler_params=_params(("arbitrary",), 40),
        name="rmsnorm",
    )(x, g.reshape(1, d))


def _matmul_kernel(x_ref, w_ref, o_ref, wb_ref):
    @pl.when(pl.program_id(1) == 0)
    def _():
        wb_ref[...] = w_ref[...].astype(BF16)

    o_ref[...] = jnp.dot(x_ref[...], wb_ref[...], preferred_element_type=F32)


def _matmul_res_kernel(x_ref, w_ref, r_ref, o_ref, wb_ref):
    @pl.when(pl.program_id(1) == 0)
    def _():
        wb_ref[...] = w_ref[...].astype(BF16)

    o_ref[...] = r_ref[...] + jnp.dot(x_ref[...], wb_ref[...], preferred_element_type=F32)


def matmul_layer(x, w_all, layer, tm, tn, residual=None):
    t, k = x.shape
    n = w_all.shape[2]
    grid = (n // tn, t // tm)
    in_specs = [pl.BlockSpec((tm, k), lambda j, i: (i, 0)),
                pl.BlockSpec((None, k, tn), lambda j, i: (layer, 0, j))]
    args = [x, w_all]
    kern = _matmul_kernel
    if residual is not None:
        in_specs.append(pl.BlockSpec((tm, tn), lambda j, i: (i, j)))
        args.append(residual)
        kern = _matmul_res_kernel
    return pl.pallas_call(
        kern,
        grid=grid,
        in_specs=in_specs,
        out_specs=pl.BlockSpec((tm, tn), lambda j, i: (i, j)),
        out_shape=jax.ShapeDtypeStruct((t, n), F32),
        scratch_shapes=[pltpu.VMEM((k, tn), BF16)],
        compiler_params=_params(("arbitrary", "arbitrary"), 48),
        name="matmul",
    )(*args)


def _band_kernel(*refs, slopes, use_sink, kv_shared, with_lse):
    if use_sink:
        sink_ref, refs = refs[0], refs[1:]
    q_ref, kp_ref, kc_ref, vp_ref, vc_ref = refs[:5]
    o_ref = refs[5]
    lse_ref = refs[6] if with_lse else None
    hg = pl.program_id(3)
    has_prev = pl.program_id(2) > 0
    i = lax.broadcasted_iota(jnp.int32, (BLOCK, 2 * BLOCK), 0)
    j = lax.broadcasted_iota(jnp.int32, (BLOCK, 2 * BLOCK), 1)
    dist = BLOCK + i - j
    valid = (dist >= 0) & (dist <= BLOCK) & ((j >= BLOCK) | has_prev)
    dist_f = dist.astype(F32)
    for hh in range(HEADS_PER_STEP):
        kv = slice(0, HEAD_DIM) if kv_shared else slice(hh * HEAD_DIM, (hh + 1) * HEAD_DIM)
        q = q_ref[:, hh * HEAD_DIM:(hh + 1) * HEAD_DIM]
        k = jnp.concatenate([kp_ref[:, kv], kc_ref[:, kv]], axis=0)
        v = jnp.concatenate([vp_ref[:, kv], vc_ref[:, kv]], axis=0)
        slope = jnp.where(hg == 0, slopes[hh], slopes[HEADS_PER_STEP + hh])
        s = _bdot_nt(q, k) * SCALE - slope * dist_f
        s = jnp.where(valid, s, NEG_INF)
        m = jnp.max(s, axis=-1, keepdims=True)
        if use_sink:
            sink = sink_ref[0, hg * HEADS_PER_STEP + hh]
            m = jnp.maximum(m, sink)
        p = jnp.exp(s - m)
        l = jnp.sum(p, axis=-1, keepdims=True)
        if use_sink:
            l = l + jnp.exp(sink - m)
        o = _bdot(p, v) / l
        o_ref[:, hh * HEAD_DIM:(hh + 1) * HEAD_DIM] = o
        if with_lse:
            lse_ref[:, hh * HEAD_DIM:(hh + 1) * HEAD_DIM] = jnp.broadcast_to(m + jnp.log(l), (BLOCK, HEAD_DIM))


def band_attention_a(proj, sink, n_batch, seq):
    nb = seq // BLOCK
    w = HEADS_PER_STEP * HEAD_DIM
    ka0, va0 = COL_KA // HEAD_DIM, COL_VA // HEAD_DIM

    def row(b, r, ib, hg):
        return b * nb + ib

    def prev(b, r, ib, hg):
        return b * nb + jnp.maximum(ib - 1, 0)

    kern = functools.partial(_band_kernel, slopes=SLOPES_A, use_sink=True, kv_shared=True, with_lse=False)
    return pl.pallas_call(
        kern,
        grid=(n_batch, 1, nb, KV_HEADS_A),
        in_specs=[pl.BlockSpec(memory_space=pltpu.SMEM),
                  pl.BlockSpec((BLOCK, w), lambda b, r, ib, hg: (row(b, r, ib, hg), hg)),
                  pl.BlockSpec((BLOCK, HEAD_DIM), lambda b, r, ib, hg: (prev(b, r, ib, hg), ka0 + hg)),
                  pl.BlockSpec((BLOCK, HEAD_DIM), lambda b, r, ib, hg: (row(b, r, ib, hg), ka0 + hg)),
                  pl.BlockSpec((BLOCK, HEAD_DIM), lambda b, r, ib, hg: (prev(b, r, ib, hg), va0 + hg)),
                  pl.BlockSpec((BLOCK, HEAD_DIM), lambda b, r, ib, hg: (row(b, r, ib, hg), va0 + hg))],
        out_specs=pl.BlockSpec((BLOCK, w), lambda b, r, ib, hg: (row(b, r, ib, hg), hg)),
        out_shape=jax.ShapeDtypeStruct((n_batch * seq, WIDTH_A), F32),
        compiler_params=_params(("arbitrary",) * 4, 32),
        name="band_attn_a",
    )(sink.reshape(1, N_HEADS_A), proj, proj, proj, proj, proj)


def band_attention_b(proj, n_batch, seq, dilation):
    r = dilation
    nb = seq // r // BLOCK
    w = HEADS_PER_STEP * HEAD_DIM
    t_all = proj.shape[0]
    folded = proj.reshape(t_all // r, r * PROJ_WIDTH)
    per_row = PROJ_WIDTH // w
    q0, k0, v0 = COL_QB // w, COL_KB // w, COL_VB // w
    n_hg = WIDTH_B // w

    def row(b, res, ib, hg):
        return b * nb + ib

    def prev(b, res, ib, hg):
        return b * nb + jnp.maximum(ib - 1, 0)

    slopes = tuple(s * r for s in SLOPES_B)
    kern = functools.partial(_band_kernel, slopes=slopes, use_sink=False, kv_shared=False, with_lse=True)
    out_sds = jax.ShapeDtypeStruct((n_batch * seq // r, r * WIDTH_B), F32)
    out_spec = pl.BlockSpec((BLOCK, w), lambda b, res, ib, hg: (row(b, res, ib, hg), res * n_hg + hg))
    o, lse = pl.pallas_call(
        kern,
        grid=(n_batch, r, nb, n_hg),
        in_specs=[pl.BlockSpec((BLOCK, w), lambda b, res, ib, hg: (row(b, res, ib, hg), res * per_row + q0 + hg)),
                  pl.BlockSpec((BLOCK, w), lambda b, res, ib, hg: (prev(b, res, ib, hg), res * per_row + k0 + hg)),
                  pl.BlockSpec((BLOCK, w), lambda b, res, ib, hg: (row(b, res, ib, hg), res * per_row + k0 + hg)),
                  pl.BlockSpec((BLOCK, w), lambda b, res, ib, hg: (prev(b, res, ib, hg), res * per_row + v0 + hg)),
                  pl.BlockSpec((BLOCK, w), lambda b, res, ib, hg: (row(b, res, ib, hg), res * per_row + v0 + hg))],
        out_specs=[out_spec, out_spec],
        out_shape=[out_sds, out_sds],
        compiler_params=_params(("arbitrary",) * 4, 32),
        name="band_attn_b",
    )(folded, folded, folded, folded, folded)
    return o.reshape(n_batch * seq, WIDTH_B), lse.reshape(n_batch * seq, WIDTH_B)


def _merge_kernel(oa_ref, o1_ref, o2_ref, o3_ref, l1_ref, l2_ref, l3_ref, ga_ref, gb_ref, out_ref):
    l1, l2, l3 = l1_ref[...], l2_ref[...], l3_ref[...]
    m = jnp.maximum(jnp.maximum(l1, l2), l3)
    e1, e2, e3 = jnp.exp(l1 - m), jnp.exp(l2 - m), jnp.exp(l3 - m)
    ob = (e1 * o1_ref[...] + e2 * o2_ref[...] + e3 * o3_ref[...]) / (e1 + e2 + e3)
    out_ref[:, :WIDTH_A] = _rms(oa_ref[...], ga_ref[...]).astype(out_ref.dtype)
    out_ref[:, WIDTH_A:] = _rms(ob, gb_ref[...]).astype(out_ref.dtype)


def merge_prompt(oa, obs, lses, g_a, g_b, row_block):
    t = oa.shape[0]
    spec = pl.BlockSpec((row_block, WIDTH_A), lambda i: (i, 0))
    gspec = pl.BlockSpec((1, WIDTH_A), lambda i: (0, 0))
    return pl.pallas_call(
        _merge_kernel,
        grid=(t // row_block,),
        in_specs=[spec] * 7 + [gspec, gspec],
        out_specs=pl.BlockSpec((row_block, WIDTH_A + WIDTH_B), lambda i: (i, 0)),
        out_shape=jax.ShapeDtypeStruct((t, WIDTH_A + WIDTH_B), BF16),
        compiler_params=_params(("arbitrary",), 40),
        name="merge_prompt",
    )(oa, *obs, *lses, g_a.reshape(1, WIDTH_A), g_b.reshape(1, WIDTH_B))


T_DEC = 4
ROWS_PAD = 8
ROWS_16 = WINDOW_B_MAX // 16
ROWS_4 = 512


def _select_by_index(idx, values):
    out = values[-1]
    for n in range(len(values) - 2, -1, -1):
        out = jnp.where(idx == n, values[n], out)
    return out


def _sample_attn_kernel(sink_ref, qa_ref, qb_ref, kn_ref, cak_ref, cav_ref, ck16_ref, cv16_ref,
                        ck4_ref, cv4_ref, ga_ref, gb_ref, o_ref):
    kn = kn_ref[0]

    def new_keys(q, col_k, col_v, t_row, slope, mult_self):
        out = []
        for tp in range(T_DEC):
            k_row = kn[tp:tp + 1, col_k:col_k + HEAD_DIM]
            v_row = kn[tp:tp + 1, col_v:col_v + HEAD_DIM]
            s = jnp.sum(q * k_row, axis=-1, keepdims=True) * SCALE - slope * (t_row - tp).astype(F32)
            wgt = jnp.where(t_row == tp, mult_self, jnp.where(t_row > tp, 1.0, 0.0))
            out.append((jnp.where(wgt > 0, s, NEG_INF), wgt, v_row))
        return out

    rows_a = GROUP_A * T_DEC
    for hk in range(KV_HEADS_A):
        q = qa_ref[0, hk * rows_a:(hk + 1) * rows_a, :]
        row = lax.broadcasted_iota(jnp.int32, (rows_a, 1), 0)
        t_row, g_row = row & (T_DEC - 1), row >> 2
        slope = _select_by_index(g_row, [SLOPES_A[hk * GROUP_A + g] for g in range(GROUP_A)])
        sink = _select_by_index(g_row, [sink_ref[0, hk * GROUP_A + g] for g in range(GROUP_A)])
        kc = cak_ref[0, 0, :, hk * HEAD_DIM:(hk + 1) * HEAD_DIM]
        vc = cav_ref[0, 0, :, hk * HEAD_DIM:(hk + 1) * HEAD_DIM]
        c = lax.broadcasted_iota(jnp.int32, (rows_a, WINDOW_A), 1)
        dist = t_row + WINDOW_A - c
        s_c = _bdot_nt(q, kc) * SCALE - slope * dist.astype(F32)
        s_c = jnp.where(dist <= WINDOW_A, s_c, NEG_INF)
        news = new_keys(q, COL_KA + hk * HEAD_DIM, COL_VA + hk * HEAD_DIM, t_row, slope, 1.0)
        m = jnp.maximum(jnp.max(s_c, axis=-1, keepdims=True), sink)
        for s_n, _, _ in news:
            m = jnp.maximum(m, s_n)
        p_c = jnp.exp(s_c - m)
        l = jnp.sum(p_c, axis=-1, keepdims=True) + jnp.exp(sink - m)
        acc = _bdot(p_c, vc)
        for s_n, wgt, v_row in news:
            p_n = wgt * jnp.exp(s_n - m)
            l = l + p_n
            acc = acc + p_n * v_row
        o = acc / l
        for g in range(GROUP_A):
            col = (hk * GROUP_A + g) * HEAD_DIM
            o_ref[0, :, col:col + HEAD_DIM] = o[g * T_DEC:(g + 1) * T_DEC, :]

    for h in range(N_HEADS_B):
        q = qb_ref[0, h * ROWS_PAD:(h + 1) * ROWS_PAD, :]
        t_row = lax.broadcasted_iota(jnp.int32, (ROWS_PAD, 1), 0)
        slope = SLOPES_B[h]
        hs = slice(h * HEAD_DIM, (h + 1) * HEAD_DIM)
        k16 = jnp.concatenate([ck16_ref[0, 0, :, t * WIDTH_B + h * HEAD_DIM:t * WIDTH_B + (h + 1) * HEAD_DIM]
                               for t in range(T_DEC)], axis=0)
        v16 = jnp.concatenate([cv16_ref[0, 0, :, t * WIDTH_B + h * HEAD_DIM:t * WIDTH_B + (h + 1) * HEAD_DIM]
                               for t in range(T_DEC)], axis=0)
        c16 = lax.broadcasted_iota(jnp.int32, (ROWS_PAD, T_DEC * ROWS_16), 1)
        d16 = WINDOW_B_MAX - 16 * (c16 & (ROWS_16 - 1))
        s16 = _bdot_nt(q, k16) * SCALE - slope * d16.astype(F32)
        w16 = jnp.where((c16 >> 7) == t_row, 1.0, 0.0)
        s16 = jnp.where(w16 > 0, s16, NEG_INF)
        k4 = ck4_ref[0, 0, :, hs]
        v4 = cv4_ref[0, 0, :, hs]
        c4 = lax.broadcasted_iota(jnp.int32, (ROWS_PAD, ROWS_4), 1)
        d4 = ROWS_4 + t_row - c4
        s4 = _bdot_nt(q, k4) * SCALE - slope * d4.astype(F32)
        w4 = jnp.where((d4 & 3) == 0, 1.0, 0.0) + jnp.where(d4 <= 128, 1.0, 0.0)
        s4 = jnp.where(w4 > 0, s4, NEG_INF)
        news = new_keys(q, COL_KB + h * HEAD_DIM, COL_VB + h * HEAD_DIM, t_row, slope, 3.0)
        m = jnp.maximum(jnp.max(s16, axis=-1, keepdims=True), jnp.max(s4, axis=-1, keepdims=True))
        for s_n, _, _ in news:
            m = jnp.maximum(m, s_n)
        p16 = w16 * jnp.exp(s16 - m)
        p4 = w4 * jnp.exp(s4 - m)
        l = jnp.sum(p16, axis=-1, keepdims=True) + jnp.sum(p4, axis=-1, keepdims=True)
        acc = _bdot(p16, v16) + _bdot(p4, v4)
        for s_n, wgt, v_row in news:
            p_n = wgt * jnp.exp(s_n - m)
            l = l + p_n
            acc = acc + p_n * v_row
        o = acc / l
        o_ref[0, :, WIDTH_A + h * HEAD_DIM:WIDTH_A + (h + 1) * HEAD_DIM] = o[:T_DEC, :]

    o_all = o_ref[0]
    o_ref[0, :, :WIDTH_A] = _rms(o_all[:, :WIDTH_A], ga_ref[...])
    o_ref[0, :, WIDTH_A:] = _rms(o_all[:, WIDTH_A:], gb_ref[...])


def sample_attention(proj_s, layer, sink, cache_a_k, cache_a_v, cache_b_k, cache_b_v, g_a, g_b):
    nseq = proj_s.shape[0] // T_DEC
    depth = cache_a_k.shape[0]
    p3 = proj_s.reshape(nseq, T_DEC, PROJ_WIDTH)
    qa = p3[:, :, :WIDTH_A].reshape(nseq, T_DEC, N_HEADS_A, HEAD_DIM).transpose(0, 2, 1, 3)
    qa = qa.reshape(nseq, N_HEADS_A * T_DEC, HEAD_DIM)
    qb = p3[:, :, COL_QB:COL_KB].reshape(nseq, T_DEC, N_HEADS_B, HEAD_DIM).transpose(0, 2, 1, 3)
    qb = jnp.pad(qb, ((0, 0), (0, 0), (0, ROWS_PAD - T_DEC), (0, 0))).reshape(nseq, N_HEADS_B * ROWS_PAD, HEAD_DIM)
    ca_k = cache_a_k.reshape(depth, nseq, WINDOW_A, KV_WIDTH_A)
    ca_v = cache_a_v.reshape(depth, nseq, WINDOW_A, KV_WIDTH_A)
    cb16_k = cache_b_k.reshape(depth, nseq, ROWS_16, 16 * WIDTH_B)
    cb16_v = cache_b_v.reshape(depth, nseq, ROWS_16, 16 * WIDTH_B)
    cb4_k = cache_b_k.reshape(depth, nseq, WINDOW_B_MAX, WIDTH_B)
    cb4_v = cache_b_v.reshape(depth, nseq, WINDOW_B_MAX, WIDTH_B)
    last4 = WINDOW_B_MAX // ROWS_4 - 1
    a_spec = pl.BlockSpec((1, 1, WINDOW_A, KV_WIDTH_A), lambda b: (layer, b, 0, 0))
    b16_spec = pl.BlockSpec((1, 1, ROWS_16, T_DEC * WIDTH_B), lambda b: (layer, b, 0, 0))
    b4_spec = pl.BlockSpec((1, 1, ROWS_4, WIDTH_B), lambda b: (layer, b, last4, 0))
    g_spec = pl.BlockSpec((1, WIDTH_A), lambda b: (0, 0))
    out = pl.pallas_call(
        _sample_attn_kernel,
        grid=(nseq,),
        in_specs=[pl.BlockSpec(memory_space=pltpu.SMEM),
                  pl.BlockSpec((1, N_HEADS_A * T_DEC, HEAD_DIM), lambda b: (b, 0, 0)),
                  pl.BlockSpec((1, N_HEADS_B * ROWS_PAD, HEAD_DIM), lambda b: (b, 0, 0)),
                  pl.BlockSpec((1, T_DEC, PROJ_WIDTH), lambda b: (b, 0, 0)),
                  a_spec, a_spec, b16_spec, b16_spec, b4_spec, b4_spec, g_spec, g_spec],
        out_specs=pl.BlockSpec((1, T_DEC, WIDTH_A + WIDTH_B), lambda b: (b, 0, 0)),
        out_shape=jax.ShapeDtypeStruct((nseq, T_DEC, WIDTH_A + WIDTH_B), F32),
        compiler_params=_params(("arbitrary",), 48),
        name="sample_attn",
    )(sink.reshape(1, N_HEADS_A), qa, qb, p3, ca_k, ca_v, cb16_k, cb16_v, cb4_k, cb4_v,
      g_a.reshape(1, WIDTH_A), g_b.reshape(1, WIDTH_B))
    return out.reshape(nseq * T_DEC, WIDTH_A + WIDTH_B)


def _swiglu_step(h, wg, wu, wd):
    g = jnp.dot(h, wg.astype(BF16), preferred_element_type=F32)
    u = jnp.dot(h, wu.astype(BF16), preferred_element_type=F32)
    a = (g * jax.nn.sigmoid(g) * u).astype(BF16)
    return jnp.dot(a, wd.astype(BF16), preferred_element_type=F32)


def _dense_ffn_kernel(h_ref, wg_ref, wu_ref, wd_ref, x_ref, o_ref):
    f = pl.program_id(1)
    y = _swiglu_step(h_ref[...], wg_ref[...], wu_ref[...], wd_ref[...])

    @pl.when(f == 0)
    def _():
        o_ref[...] = x_ref[...] + y

    @pl.when(f > 0)
    def _():
        o_ref[...] += y


def dense_ffn(h, x, w_gate, w_up, w_down, idx, tm, tf):
    t, d = h.shape
    d_ff = w_gate.shape[2]
    return pl.pallas_call(
        _dense_ffn_kernel,
        grid=(t // tm, d_ff // tf),
        in_specs=[pl.BlockSpec((tm, d), lambda i, f: (i, 0)),
                  pl.BlockSpec((None, d, tf), lambda i, f: (idx, 0, f)),
                  pl.BlockSpec((None, d, tf), lambda i, f: (idx, 0, f)),
                  pl.BlockSpec((None, tf, d), lambda i, f: (idx, f, 0)),
                  pl.BlockSpec((tm, d), lambda i, f: (i, 0))],
        out_specs=pl.BlockSpec((tm, d), lambda i, f: (i, 0)),
        out_shape=jax.ShapeDtypeStruct((t, d), F32),
        compiler_params=_params(("arbitrary", "arbitrary"), 56),
        name="dense_ffn",
    )(h, w_gate, w_up, w_down, x)


ROUTE_LANES = 128


def _router_kernel(x_ref, g_ref, wr_ref, h_ref, route_ref):
    h = _rms(x_ref[...], g_ref[...])
    h_ref[...] = h
    logits = jnp.dot(h, wr_ref[...], preferred_element_type=F32, precision=lax.Precision.HIGHEST)
    lane = lax.broadcasted_iota(jnp.int32, logits.shape, 1).astype(F32)
    lg = jnp.where(lane < N_EXPERTS, logits, -jnp.inf)
    m1 = jnp.max(lg, axis=-1, keepdims=True)
    i1 = jnp.min(jnp.where(lg == m1, lane, float(ROUTE_LANES)), axis=-1, keepdims=True)
    lg2 = jnp.where(lane == i1, -jnp.inf, lg)
    m2 = jnp.max(lg2, axis=-1, keepdims=True)
    i2 = jnp.min(jnp.where(lg2 == m2, lane, float(ROUTE_LANES)), axis=-1, keepdims=True)
    e = jnp.exp(m2 - m1)
    g1 = 1.0 / (1.0 + e)
    g2 = e / (1.0 + e)
    route_ref[...] = jnp.where(lane == 0, i1, jnp.where(lane == 1, i2,
                                                        jnp.where(lane == 2, g1, jnp.where(lane == 3, g2, 0.0))))


def router(x, g, w_router, tm):
    t, d = x.shape
    wr = jnp.pad(w_router, ((0, 0), (0, ROUTE_LANES - N_EXPERTS)))
    h, route = pl.pallas_call(
        _router_kernel,
        grid=(t // tm,),
        in_specs=[pl.BlockSpec((tm, d), lambda i: (i, 0)),
                  pl.BlockSpec((1, d), lambda i: (0, 0)),
                  pl.BlockSpec((d, ROUTE_LANES), lambda i: (0, 0))],
        out_specs=[pl.BlockSpec((tm, d), lambda i: (i, 0)),
                   pl.BlockSpec((tm, ROUTE_LANES), lambda i: (i, 0))],
        out_shape=[jax.ShapeDtypeStruct((t, d), F32), jax.ShapeDtypeStruct((t, ROUTE_LANES), F32)],
        compiler_params=_params(("arbitrary",), 40),
        name="router",
    )(x, g.reshape(1, d), wr)
    top_i = route[:, :TOP_K].astype(jnp.int32)
    gates = route[:, TOP_K:2 * TOP_K]
    return h, top_i, gates


def _row_copy(src_ref, dst_ref, src_row, dst_row, sem):
    return pltpu.make_async_copy(src_ref.at[pl.ds(src_row, 1)], dst_ref.at[pl.ds(dst_row, 1)], sem)


def _gather_kernel(idx_ref, src_ref, o_ref, sem):
    rows = o_ref.shape[0]
    base = pl.program_id(0) * rows

    def start(r, carry):
        _row_copy(src_ref, o_ref, idx_ref[base + r], r, sem).start()
        return carry

    def wait(r, carry):
        _row_copy(src_ref, o_ref, idx_ref[base + r], r, sem).wait()
        return carry

    lax.fori_loop(0, rows, start, 0)
    lax.fori_loop(0, rows, wait, 0)


def gather_rows(src, idx, rows_per_step):
    n = idx.shape[0]
    d = src.shape[1]
    return pl.pallas_call(
        _gather_kernel,
        grid_spec=pltpu.PrefetchScalarGridSpec(
            num_scalar_prefetch=1,
            grid=(n // rows_per_step,),
            in_specs=[pl.BlockSpec(memory_space=pl.ANY)],
            out_specs=pl.BlockSpec((rows_per_step, d), lambda i, idx_ref: (i, 0)),
            scratch_shapes=[pltpu.SemaphoreType.DMA(())]),
        out_shape=jax.ShapeDtypeStruct((n, d), src.dtype),
        compiler_params=_params(("arbitrary",), 32),
        name="gather_rows",
    )(idx, src)


def _moe_kernel(te_ref, tv_ref, xs_ref, wg_ref, wu_ref, wd_ref, o_ref, xb_ref):
    t = pl.program_id(0)
    f = pl.program_id(1)
    live = tv_ref[t] > 0

    @pl.when(live & (f == 0))
    def _():
        xb_ref[...] = xs_ref[...].astype(BF16)

    @pl.when(jnp.logical_not(live) & (f == 0))
    def _():
        o_ref[...] = jnp.zeros_like(o_ref)

    @pl.when(live)
    def _():
        y = _swiglu_step(xb_ref[...], wg_ref[...], wu_ref[...], wd_ref[...])

        @pl.when(f == 0)
        def _():
            o_ref[...] = y

        @pl.when(f > 0)
        def _():
            o_ref[...] += y


def moe_experts(xs, tile_expert, tile_live, w_gate, w_up, w_down, idx, tm, tf):
    p, d = xs.shape
    d_ff = w_gate.shape[3]
    nf = d_ff // tf

    def f_of(t, f, tv):
        return jnp.where(tv[t] > 0, f, nf - 1)

    return pl.pallas_call(
        _moe_kernel,
        grid_spec=pltpu.PrefetchScalarGridSpec(
            num_scalar_prefetch=2,
            grid=(p // tm, nf),
            in_specs=[pl.BlockSpec((tm, d), lambda t, f, te, tv: (t, 0)),
                      pl.BlockSpec((None, None, d, tf), lambda t, f, te, tv: (idx, te[t], 0, f_of(t, f, tv))),
                      pl.BlockSpec((None, None, d, tf), lambda t, f, te, tv: (idx, te[t], 0, f_of(t, f, tv))),
                      pl.BlockSpec((None, None, tf, d), lambda t, f, te, tv: (idx, te[t], f_of(t, f, tv), 0))],
            out_specs=pl.BlockSpec((tm, d), lambda t, f, te, tv: (t, 0)),
            scratch_shapes=[pltpu.VMEM((tm, d), BF16)]),
        out_shape=jax.ShapeDtypeStruct((p, d), F32),
        compiler_params=_params(("arbitrary", "arbitrary"), 56),
        name="moe_experts",
    )(tile_expert, tile_live, xs, w_gate, w_up, w_down)


def _combine_kernel(pos_ref, x_ref, gate_ref, ys_ref, o_ref, buf_ref, sem):
    rows = x_ref.shape[0]
    base = pl.program_id(0) * rows

    def copies(r):
        return [_row_copy(ys_ref, buf_ref.at[k], pos_ref[TOP_K * (base + r) + k], r, sem) for k in range(TOP_K)]

    def start(r, carry):
        for cp in copies(r):
            cp.start()
        return carry

    def wait(r, carry):
        for cp in copies(r):
            cp.wait()
        return carry

    lax.fori_loop(0, rows, start, 0)
    lax.fori_loop(0, rows, wait, 0)
    gates = gate_ref[...]
    y = x_ref[...]
    for k in range(TOP_K):
        y = y + gates[:, k:k + 1] * buf_ref[k]
    o_ref[...] = y


def combine_rows(x, gates, ys, pos, rows_per_step):
    t, d = x.shape
    return pl.pallas_call(
        _combine_kernel,
        grid_spec=pltpu.PrefetchScalarGridSpec(
            num_scalar_prefetch=1,
            grid=(t // rows_per_step,),
            in_specs=[pl.BlockSpec((rows_per_step, d), lambda i, pos_ref: (i, 0)),
                      pl.BlockSpec((rows_per_step, TOP_K), lambda i, pos_ref: (i, 0)),
                      pl.BlockSpec(memory_space=pl.ANY)],
            out_specs=pl.BlockSpec((rows_per_step, d), lambda i, pos_ref: (i, 0)),
            scratch_shapes=[pltpu.VMEM((TOP_K, rows_per_step, d), F32), pltpu.SemaphoreType.DMA(())]),
        out_shape=jax.ShapeDtypeStruct((t, d), F32),
        compiler_params=_params(("arbitrary",), 32),
        name="combine_rows",
    )(pos, x, gates, ys)


def moe_ffn(x, g, w_router, w_gate, w_up, w_down, idx, tm, tf):
    t = x.shape[0]
    h, top_i, gates = router(x, g, w_router, 640)
    ids = top_i.reshape(-1)
    onehot = (ids[:, None] == jnp.arange(N_EXPERTS)[None, :]).astype(jnp.int32)
    counts = jnp.sum(onehot, axis=0)
    rank = jnp.take_along_axis(jnp.cumsum(onehot, axis=0) - onehot, ids[:, None], axis=1)[:, 0]
    tiles = (counts + tm - 1) // tm
    tile_end = jnp.cumsum(tiles)
    group_start = (tile_end - tiles) * tm
    pos = group_start[ids] + rank
    n_tiles = (TOP_K * t) // tm + N_EXPERTS
    slot_row = jnp.zeros((n_tiles * tm,), jnp.int32).at[pos].set(jnp.arange(TOP_K * t, dtype=jnp.int32) // TOP_K)
    tile_idx = jnp.arange(n_tiles, dtype=jnp.int32)
    tile_live = (tile_idx < tile_end[-1]).astype(jnp.int32)
    tile_expert = jnp.minimum(jnp.sum((tile_idx[:, None] >= tile_end[None, :]).astype(jnp.int32), axis=1),
                              N_EXPERTS - 1)
    last_live = jnp.take(tile_expert, jnp.maximum(tile_end[-1] - 1, 0))
    tile_expert = jnp.where(tile_live > 0, tile_expert, last_live).astype(jnp.int32)
    xs = gather_rows(h, slot_row, 256)
    ys = moe_experts(xs, tile_expert, tile_live, w_gate, w_up, w_down, idx, tm, tf)
    return combine_rows(x, gates, ys, pos.astype(jnp.int32), 128)


def kernel(x_prompt, x_sample, cache_a_k, cache_a_v, cache_b_k, cache_b_v, g_mix, w_in, sink_a, g_out_a, g_out_b,
           w_out, g_ffn, w_gate_dense, w_up_dense, w_down_dense, w_router, w_gate_moe, w_up_moe, w_down_moe,
           g_final):
    n_batch, seq, d = x_prompt.shape
    n_seq, t_dec, _ = x_sample.shape
    depth = w_in.shape[0]
    tp = n_batch * seq
    ts = n_seq * t_dec
    t_all = tp + ts
    tm = 640
    assert t_all % tm == 0 and t_dec == T_DEC and tp % BLOCK == 0 and ts == BLOCK
    keep_a, keep_b = min(WINDOW_A, seq), min(WINDOW_B_MAX, seq)
    x = jnp.concatenate([x_prompt.reshape(tp, d), x_sample.reshape(ts, d)], axis=0)
    outs = [[] for _ in range(8)]
    for l in range(depth):
        h = rmsnorm_rows(x, g_mix[l], BF16, tm, 0, t_all // tm)
        proj = matmul_layer(h, w_in, l, tm, 768)
        proj_s = proj[tp:]
        oa = band_attention_a(proj, sink_a[l], n_batch, seq)
        obs, lses = zip(*[band_attention_b(proj, n_batch, seq, r) for _, r in DILATED_BRANCHES])
        o_p = merge_prompt(oa, obs, lses, g_out_a[l], g_out_b[l], 256)
        o_s = sample_attention(proj_s, l, sink_a[l], cache_a_k, cache_a_v, cache_b_k, cache_b_v,
                               g_out_a[l], g_out_b[l])
        o_all = jnp.concatenate([o_p, o_s.astype(BF16)], axis=0)
        x = matmul_layer(o_all, w_out, l, tm, 512, residual=x)
        if l % 2 == 0:
            h2 = rmsnorm_rows(x, g_ffn[l], BF16, tm, 0, t_all // tm)
            x = dense_ffn(h2, x, w_gate_dense, w_up_dense, w_down_dense, l // 2, tm, 256)
        else:
            x = moe_ffn(x, g_ffn[l], w_router[l // 2], w_gate_moe, w_up_moe, w_down_moe, l // 2, 512, 512)
        pp = proj[:tp].reshape(n_batch, seq, PROJ_WIDTH)
        ps = proj_s.reshape(n_seq, t_dec, PROJ_WIDTH)
        outs[0].append(pp[:, seq - keep_a:, COL_KA:COL_VA].reshape(n_batch, keep_a, KV_HEADS_A, HEAD_DIM))
        outs[1].append(pp[:, seq - keep_a:, COL_VA:COL_QB].reshape(n_batch, keep_a, KV_HEADS_A, HEAD_DIM))
        outs[2].append(pp[:, seq - keep_b:, COL_KB:COL_VB].reshape(n_batch, keep_b, N_HEADS_B, HEAD_DIM))
        outs[3].append(pp[:, seq - keep_b:, COL_VB:].reshape(n_batch, keep_b, N_HEADS_B, HEAD_DIM))
        ka_s = ps[:, :, COL_KA:COL_VA].reshape(n_seq, t_dec, KV_HEADS_A, HEAD_DIM)
        va_s = ps[:, :, COL_VA:COL_QB].reshape(n_seq, t_dec, KV_HEADS_A, HEAD_DIM)
        kb_s = ps[:, :, COL_KB:COL_VB].reshape(n_seq, t_dec, N_HEADS_B, HEAD_DIM)
        vb_s = ps[:, :, COL_VB:].reshape(n_seq, t_dec, N_HEADS_B, HEAD_DIM)
        outs[4].append(jnp.concatenate([cache_a_k[l][:, t_dec:], ka_s], axis=1))
        outs[5].append(jnp.concatenate([cache_a_v[l][:, t_dec:], va_s], axis=1))
        outs[6].append(jnp.concatenate([cache_b_k[l][:, t_dec:], kb_s], axis=1))
        outs[7].append(jnp.concatenate([cache_b_v[l][:, t_dec:], vb_s], axis=1))
    y_prompt = rmsnorm_rows(x, g_final, F32, 512, 0, tp // 512).reshape(n_batch, seq, d)
    y_sample = rmsnorm_rows(x, g_final, F32, ts, tp // ts, 1).reshape(n_seq, t_dec, d)
    return (y_prompt, y_sample) + tuple(jnp.stack(o) for o in outs)
```

```python
import functools

import jax
import jax.numpy as jnp
import numpy as np
from jax import lax
from jax.experimental import pallas as pl
from jax.experimental.pallas import tpu as pltpu

F32 = jnp.float32
BF16 = jnp.bfloat16

D_MODEL = 2048
HEAD_DIM = 128
N_HEADS_A = 8
KV_HEADS_A = 2
GROUP_A = 4
N_HEADS_B = 8
WIDTH_A = N_HEADS_A * HEAD_DIM
KV_WIDTH_A = KV_HEADS_A * HEAD_DIM
WIDTH_B = N_HEADS_B * HEAD_DIM
PROJ_WIDTH = WIDTH_A + 2 * KV_WIDTH_A + 3 * WIDTH_B
COL_KA = WIDTH_A
COL_VA = COL_KA + KV_WIDTH_A
COL_QB = COL_VA + KV_WIDTH_A
COL_KB = COL_QB + WIDTH_B
COL_VB = COL_KB + WIDTH_B
WINDOW_A = 128
DILATED_BRANCHES = ((128, 1), (512, 4), (2048, 16))
WINDOW_B_MAX = 2048
BLOCK = 128
N_EXPERTS = 8
TOP_K = 2
EPS = 1e-5
NEG_INF = -1e30
SCALE = HEAD_DIM ** -0.5
SLOPES = tuple(2.0 ** (-8.0 * i / (N_HEADS_A + N_HEADS_B)) for i in range(1, N_HEADS_A + N_HEADS_B + 1))
SLOPES_A = SLOPES[:N_HEADS_A]
SLOPES_B = SLOPES[N_HEADS_A:]

MIB = 1024 * 1024


def _params(semantics, vmem_mib):
    return pltpu.CompilerParams(dimension_semantics=semantics, vmem_limit_bytes=vmem_mib * MIB)


def _bdot(a, b):
    return jnp.dot(a.astype(BF16), b.astype(BF16), preferred_element_type=F32)


def _bdot_nt(a, b):
    return lax.dot_general(a.astype(BF16), b.astype(BF16), (((1,), (1,)), ((), ())),
                           preferred_element_type=F32)


def _rms(x, g):
    return x * lax.rsqrt(jnp.mean(x * x, axis=-1, keepdims=True) + EPS) * g


def _rmsnorm_kernel(x_ref, g_ref, o_ref):
    o_ref[...] = _rms(x_ref[...], g_ref[...]).astype(o_ref.dtype)


def rmsnorm_rows(x, g, out_dtype, row_block, first_block, n_blocks):
    d = x.shape[1]
    return pl.pallas_call(
        _rmsnorm_kernel,
        grid=(n_blocks,),
        in_specs=[pl.BlockSpec((row_block, d), lambda i: (i + first_block, 0)),
                  pl.BlockSpec((1, d), lambda i: (0, 0))],
        out_specs=pl.BlockSpec((row_block, d), lambda i: (i, 0)),
        out_shape=jax.ShapeDtypeStruct((n_blocks * row_block, d), out_dtype),
        compiler_params=_params(("arbitrary",), 40),
        name="rmsnorm",
    )(x, g.reshape(1, d))


def _matmul_kernel(x_ref, w_ref, o_ref, wb_ref):
    @pl.when(pl.program_id(1) == 0)
    def _():
        wb_ref[...] = w_ref[...].astype(BF16)

    o_ref[...] = jnp.dot(x_ref[...], wb_ref[...], preferred_element_type=F32)


def _matmul_res_kernel(x_ref, w_ref, r_ref, o_ref, wb_ref):
    @pl.when(pl.program_id(1) == 0)
    def _():
        wb_ref[...] = w_ref[...].astype(BF16)

    o_ref[...] = r_ref[...] + jnp.dot(x_ref[...], wb_ref[...], preferred_element_type=F32)


def matmul_layer(x, w_all, layer, tm, tn, residual=None):
    t, k = x.shape
    n = w_all.shape[2]
    grid = (n // tn, t // tm)
    in_specs = [pl.BlockSpec((tm, k), lambda j, i: (i, 0)),
                pl.BlockSpec((None, k, tn), lambda j, i: (layer, 0, j))]
    args = [x, w_all]
    kern = _matmul_kernel
    if residual is not None:
        in_specs.append(pl.BlockSpec((tm, tn), lambda j, i: (i, j)))
        args.append(residual)
        kern = _matmul_res_kernel
    return pl.pallas_call(
        kern,
        grid=grid,
        in_specs=in_specs,
        out_specs=pl.BlockSpec((tm, tn), lambda j, i: (i, j)),
        out_shape=jax.ShapeDtypeStruct((t, n), F32),
        scratch_shapes=[pltpu.VMEM((k, tn), BF16)],
        compiler_params=_params(("arbitrary", "arbitrary"), 48),
        name="matmul",
    )(*args)


def _band_mask(has_prev):
    i = lax.broadcasted_iota(jnp.int32, (BLOCK, 2 * BLOCK), 0)
    j = lax.broadcasted_iota(jnp.int32, (BLOCK, 2 * BLOCK), 1)
    dist = BLOCK + i - j
    valid = (dist >= 0) & (dist <= BLOCK) & ((j >= BLOCK) | has_prev)
    return dist.astype(F32), valid


def _band_head(q, kp, kc, vp, vc, slope, dist_f, valid, sink):
    k = jnp.concatenate([kp, kc], axis=0)
    v = jnp.concatenate([vp, vc], axis=0)
    s = _bdot_nt(q, k) * SCALE - slope * dist_f
    s = jnp.where(valid, s, NEG_INF)
    m = jnp.max(s, axis=-1, keepdims=True)
    if sink is not None:
        m = jnp.maximum(m, sink)
    p = jnp.exp(s - m)
    l = jnp.sum(p, axis=-1, keepdims=True)
    if sink is not None:
        l = l + jnp.exp(sink - m)
    return _bdot(p, v) / l, m + jnp.log(l)


def _band_a_kernel(sink_ref, q_ref, kp_ref, kc_ref, vp_ref, vc_ref, o_ref):
    hk = pl.program_id(2)
    dist_f, valid = _band_mask(pl.program_id(1) > 0)
    for g in range(GROUP_A):
        cols = slice(g * HEAD_DIM, (g + 1) * HEAD_DIM)
        slope = jnp.where(hk == 0, SLOPES_A[g], SLOPES_A[GROUP_A + g])
        o, _ = _band_head(q_ref[:, cols], kp_ref[...], kc_ref[...], vp_ref[...], vc_ref[...], slope,
                          dist_f, valid, sink_ref[0, hk * GROUP_A + g])
        o_ref[:, cols] = o


def band_attention_a(proj, sink, n_batch, seq):
    nb = seq // BLOCK
    w = GROUP_A * HEAD_DIM
    ka0, va0 = COL_KA // HEAD_DIM, COL_VA // HEAD_DIM

    def row(b, ib):
        return b * nb + ib

    def prev(b, ib):
        return b * nb + jnp.maximum(ib - 1, 0)

    return pl.pallas_call(
        _band_a_kernel,
        grid=(n_batch, nb, KV_HEADS_A),
        in_specs=[pl.BlockSpec(memory_space=pltpu.SMEM),
                  pl.BlockSpec((BLOCK, w), lambda b, ib, hk: (row(b, ib), hk)),
                  pl.BlockSpec((BLOCK, HEAD_DIM), lambda b, ib, hk: (prev(b, ib), ka0 + hk)),
                  pl.BlockSpec((BLOCK, HEAD_DIM), lambda b, ib, hk: (row(b, ib), ka0 + hk)),
                  pl.BlockSpec((BLOCK, HEAD_DIM), lambda b, ib, hk: (prev(b, ib), va0 + hk)),
                  pl.BlockSpec((BLOCK, HEAD_DIM), lambda b, ib, hk: (row(b, ib), va0 + hk))],
        out_specs=pl.BlockSpec((BLOCK, w), lambda b, ib, hk: (row(b, ib), hk)),
        out_shape=jax.ShapeDtypeStruct((n_batch * seq, WIDTH_A), F32),
        compiler_params=_params(("arbitrary",) * 3, 32),
        name="band_attn_a",
    )(sink.reshape(1, N_HEADS_A), proj, proj, proj, proj, proj)


def _band_b_kernel(slope_ref, q_ref, kp_ref, kc_ref, vp_ref, vc_ref, o_ref, lse_ref, *, dilation, heads):
    hg = pl.program_id(1)
    dist_f, valid = _band_mask(pl.program_id(2) > 0)

    def residue(rows):
        for hh in range(heads):
            cols = slice(hh * HEAD_DIM, (hh + 1) * HEAD_DIM)
            o, lse = _band_head(q_ref[rows, cols], kp_ref[rows, cols], kc_ref[rows, cols], vp_ref[rows, cols],
                                vc_ref[rows, cols], slope_ref[0, hg * heads + hh], dist_f, valid, None)
            o_ref[rows, cols] = o
            lse_ref[rows, cols] = jnp.broadcast_to(lse, (BLOCK, HEAD_DIM))

    if dilation == 1:
        residue(slice(None))
    else:
        def body(res, carry):
            residue(pl.ds(res, BLOCK, stride=dilation))
            return carry

        lax.fori_loop(0, dilation, body, 0)


def band_attention_b(proj, n_batch, seq, dilation, heads):
    r = dilation
    chunk = BLOCK * r
    nc = seq // chunk
    w = heads * HEAD_DIM
    q0, k0, v0 = COL_QB // w, COL_KB // w, COL_VB // w

    def row(b, ic):
        return b * nc + ic

    def prev(b, ic):
        return b * nc + jnp.maximum(ic - 1, 0)

    slopes = jnp.asarray([[s * r for s in SLOPES_B]], F32)
    kern = functools.partial(_band_b_kernel, dilation=r, heads=heads)
    out_sds = jax.ShapeDtypeStruct((n_batch * seq, WIDTH_B), F32)
    out_spec = pl.BlockSpec((chunk, w), lambda b, hg, ic: (row(b, ic), hg))
    return pl.pallas_call(
        kern,
        grid=(n_batch, WIDTH_B // w, nc),
        in_specs=[pl.BlockSpec(memory_space=pltpu.SMEM),
                  pl.BlockSpec((chunk, w), lambda b, hg, ic: (row(b, ic), q0 + hg)),
                  pl.BlockSpec((chunk, w), lambda b, hg, ic: (prev(b, ic), k0 + hg)),
                  pl.BlockSpec((chunk, w), lambda b, hg, ic: (row(b, ic), k0 + hg)),
                  pl.BlockSpec((chunk, w), lambda b, hg, ic: (prev(b, ic), v0 + hg)),
                  pl.BlockSpec((chunk, w), lambda b, hg, ic: (row(b, ic), v0 + hg))],
        out_specs=[out_spec, out_spec],
        out_shape=[out_sds, out_sds],
        compiler_params=_params(("arbitrary",) * 3, 40),
        name="band_attn_b",
    )(slopes, proj, proj, proj, proj, proj)


def _merge_kernel(oa_ref, o1_ref, o2_ref, o3_ref, l1_ref, l2_ref, l3_ref, ga_ref, gb_ref, out_ref):
    l1, l2, l3 = l1_ref[...], l2_ref[...], l3_ref[...]
    m = jnp.maximum(jnp.maximum(l1, l2), l3)
    e1, e2, e3 = jnp.exp(l1 - m), jnp.exp(l2 - m), jnp.exp(l3 - m)
    ob = (e1 * o1_ref[...] + e2 * o2_ref[...] + e3 * o3_ref[...]) / (e1 + e2 + e3)
    out_ref[:, :WIDTH_A] = _rms(oa_ref[...], ga_ref[...]).astype(out_ref.dtype)
    out_ref[:, WIDTH_A:] = _rms(ob, gb_ref[...]).astype(out_ref.dtype)


def merge_prompt(oa, obs, lses, g_a, g_b, row_block):
    t = oa.shape[0]
    spec = pl.BlockSpec((row_block, WIDTH_A), lambda i: (i, 0))
    gspec = pl.BlockSpec((1, WIDTH_A), lambda i: (0, 0))
    return pl.pallas_call(
        _merge_kernel,
        grid=(t // row_block,),
        in_specs=[spec] * 7 + [gspec, gspec],
        out_specs=pl.BlockSpec((row_block, WIDTH_A + WIDTH_B), lambda i: (i, 0)),
        out_shape=jax.ShapeDtypeStruct((t, WIDTH_A + WIDTH_B), BF16),
        compiler_params=_params(("arbitrary",), 40),
        name="merge_prompt",
    )(oa, *obs, *lses, g_a.reshape(1, WIDTH_A), g_b.reshape(1, WIDTH_B))


T_DEC = 4
ROWS_16 = WINDOW_B_MAX // 16
ROWS_4 = 512


def _select_by_index(idx, values):
    out = values[-1]
    for n in range(len(values) - 2, -1, -1):
        out = jnp.where(idx == n, values[n], out)
    return out


def _sample_group_a(sink_ref, qa_ref, kn_ref, cak_ref, cav_ref, ga_ref, oa_ref):
    kn = kn_ref[0]
    rows_a = GROUP_A * T_DEC
    for hk in range(KV_HEADS_A):
        q = qa_ref[0, hk * rows_a:(hk + 1) * rows_a, :]
        row = lax.broadcasted_iota(jnp.int32, (rows_a, 1), 0)
        t_row, g_row = row & (T_DEC - 1), row >> 2
        slope = _select_by_index(g_row, [SLOPES_A[hk * GROUP_A + g] for g in range(GROUP_A)])
        sink = _select_by_index(g_row, [sink_ref[0, hk * GROUP_A + g] for g in range(GROUP_A)])
        kc = cak_ref[0, 0, :, hk * HEAD_DIM:(hk + 1) * HEAD_DIM]
        vc = cav_ref[0, 0, :, hk * HEAD_DIM:(hk + 1) * HEAD_DIM]
        c = lax.broadcasted_iota(jnp.int32, (rows_a, WINDOW_A), 1)
        dist = t_row + WINDOW_A - c
        s_c = _bdot_nt(q, kc) * SCALE - slope * dist.astype(F32)
        s_c = jnp.where(dist <= WINDOW_A, s_c, NEG_INF)
        news = []
        for tp in range(T_DEC):
            k_row = kn[tp:tp + 1, COL_KA + hk * HEAD_DIM:COL_KA + (hk + 1) * HEAD_DIM]
            v_row = kn[tp:tp + 1, COL_VA + hk * HEAD_DIM:COL_VA + (hk + 1) * HEAD_DIM]
            s = jnp.sum(q * k_row, axis=-1, keepdims=True) * SCALE - slope * (t_row - tp).astype(F32)
            news.append((jnp.where(t_row >= tp, s, NEG_INF), v_row))
        m = jnp.maximum(jnp.max(s_c, axis=-1, keepdims=True), sink)
        for s_n, _ in news:
            m = jnp.maximum(m, s_n)
        p_c = jnp.exp(s_c - m)
        l = jnp.sum(p_c, axis=-1, keepdims=True) + jnp.exp(sink - m)
        acc = _bdot(p_c, vc)
        for s_n, v_row in news:
            p_n = jnp.exp(s_n - m)
            l = l + p_n
            acc = acc + p_n * v_row
        o = acc / l
        for g in range(GROUP_A):
            col = (hk * GROUP_A + g) * HEAD_DIM
            oa_ref[0, :, col:col + HEAD_DIM] = o[g * T_DEC:(g + 1) * T_DEC, :]
    oa_ref[0] = _rms(oa_ref[0], ga_ref[...])


def _sample_group_b(qb_ref, knb_ref, vnb_ref, ck16_ref, cv16_ref, ck4_ref, cv4_ref, gb_ref, ob_ref):
    head = lax.broadcasted_iota(jnp.int32, (1, N_HEADS_B, 1), 1)
    slope = _select_by_index(head, list(SLOPES_B))
    i3 = lax.broadcasted_iota(jnp.int32, (ROWS_16, 1, 1), 0)
    i3f = i3.astype(F32)
    for t in range(T_DEC):
        q = qb_ref[0, t] * SCALE

        def scores(k3, dist3):
            return jnp.sum(k3 * q[None], axis=-1, keepdims=True) - slope * dist3

        s16 = scores(ck16_ref[0, 0, :, t], WINDOW_B_MAX - 16.0 * i3f)
        s4 = scores(ck4_ref[0, 0, pl.ds(t, ROWS_16, stride=4)], ROWS_4 - 4.0 * i3f)
        s1 = scores(ck4_ref[0, 0, ROWS_4 - BLOCK:ROWS_4], BLOCK + t - i3f)
        s1 = jnp.where(i3 >= t, s1, NEG_INF)
        news = []
        for tp in range(t + 1):
            s = jnp.sum(knb_ref[0, tp] * q, axis=-1, keepdims=True) - slope[0] * float(t - tp)
            news.append((s, 3.0 if tp == t else 1.0, vnb_ref[0, tp]))
        m = jnp.maximum(jnp.maximum(jnp.max(s16, axis=0), jnp.max(s4, axis=0)), jnp.max(s1, axis=0))
        for s_n, _, _ in news:
            m = jnp.maximum(m, s_n)
        p16, p4, p1 = jnp.exp(s16 - m[None]), jnp.exp(s4 - m[None]), jnp.exp(s1 - m[None])
        l = jnp.sum(p16, axis=0) + jnp.sum(p4, axis=0) + jnp.sum(p1, axis=0)
        acc = (jnp.sum(p16 * cv16_ref[0, 0, :, t], axis=0)
               + jnp.sum(p4 * cv4_ref[0, 0, pl.ds(t, ROWS_16, stride=4)], axis=0)
               + jnp.sum(p1 * cv4_ref[0, 0, ROWS_4 - BLOCK:ROWS_4], axis=0))
        for s_n, wgt, v_new in news:
            p_n = wgt * jnp.exp(s_n - m)
            l = l + p_n
            acc = acc + p_n * v_new
        o = acc / l
        ms = jnp.sum(jnp.sum(o * o, axis=1, keepdims=True), axis=0, keepdims=True) / WIDTH_B
        ob_ref[0, t] = o * lax.rsqrt(ms + EPS) * gb_ref[...]


def _sample_attn_kernel(sink_ref, qa_ref, kn_ref, cak_ref, cav_ref, qb_ref, knb_ref, vnb_ref, ck16_ref, cv16_ref,
                        ck4_ref, cv4_ref, ga_ref, gb_ref, oa_ref, ob_ref):
    _sample_group_a(sink_ref, qa_ref, kn_ref, cak_ref, cav_ref, ga_ref, oa_ref)
    _sample_group_b(qb_ref, knb_ref, vnb_ref, ck16_ref, cv16_ref, ck4_ref, cv4_ref, gb_ref, ob_ref)


def sample_attention(proj_s, layer, sink, cache_a_k, cache_a_v, cache_b_k, cache_b_v, g_a, g_b):
    nseq = proj_s.shape[0] // T_DEC
    depth = cache_a_k.shape[0]
    p3 = proj_s.reshape(nseq, T_DEC, PROJ_WIDTH)
    qa = p3[:, :, :WIDTH_A].reshape(nseq, T_DEC, N_HEADS_A, HEAD_DIM).transpose(0, 2, 1, 3)
    qa = qa.reshape(nseq, N_HEADS_A * T_DEC, HEAD_DIM)
    qb = p3[:, :, COL_QB:COL_KB].reshape(nseq, T_DEC, N_HEADS_B, HEAD_DIM)
    knb = p3[:, :, COL_KB:COL_VB].reshape(nseq, T_DEC, N_HEADS_B, HEAD_DIM)
    vnb = p3[:, :, COL_VB:].reshape(nseq, T_DEC, N_HEADS_B, HEAD_DIM)
    ca_k = cache_a_k.reshape(depth, nseq, WINDOW_A, KV_WIDTH_A)
    ca_v = cache_a_v.reshape(depth, nseq, WINDOW_A, KV_WIDTH_A)
    cb16_k = cache_b_k.reshape(depth, nseq, ROWS_16, 16, N_HEADS_B, HEAD_DIM)
    cb16_v = cache_b_v.reshape(depth, nseq, ROWS_16, 16, N_HEADS_B, HEAD_DIM)
    last4 = WINDOW_B_MAX // ROWS_4 - 1
    a_spec = pl.BlockSpec((1, 1, WINDOW_A, KV_WIDTH_A), lambda b: (layer, b, 0, 0))
    b16_spec = pl.BlockSpec((1, 1, ROWS_16, T_DEC, N_HEADS_B, HEAD_DIM), lambda b: (layer, b, 0, 0, 0, 0))
    b4_spec = pl.BlockSpec((1, 1, ROWS_4, N_HEADS_B, HEAD_DIM), lambda b: (layer, b, last4, 0, 0))
    tok_spec = pl.BlockSpec((1, T_DEC, N_HEADS_B, HEAD_DIM), lambda b: (b, 0, 0, 0))
    oa, ob = pl.pallas_call(
        _sample_attn_kernel,
        grid=(nseq,),
        in_specs=[pl.BlockSpec(memory_space=pltpu.SMEM),
                  pl.BlockSpec((1, N_HEADS_A * T_DEC, HEAD_DIM), lambda b: (b, 0, 0)),
                  pl.BlockSpec((1, T_DEC, PROJ_WIDTH), lambda b: (b, 0, 0)),
                  a_spec, a_spec, tok_spec, tok_spec, tok_spec, b16_spec, b16_spec, b4_spec, b4_spec,
                  pl.BlockSpec((1, WIDTH_A), lambda b: (0, 0)),
                  pl.BlockSpec((N_HEADS_B, HEAD_DIM), lambda b: (0, 0))],
        out_specs=[pl.BlockSpec((1, T_DEC, WIDTH_A), lambda b: (b, 0, 0)), tok_spec],
        out_shape=[jax.ShapeDtypeStruct((nseq, T_DEC, WIDTH_A), F32),
                   jax.ShapeDtypeStruct((nseq, T_DEC, N_HEADS_B, HEAD_DIM), F32)],
        compiler_params=_params(("arbitrary",), 48),
        name="sample_attn",
    )(sink.reshape(1, N_HEADS_A), qa, p3, ca_k, ca_v, qb, knb, vnb, cb16_k, cb16_v, cache_b_k, cache_b_v,
      g_a.reshape(1, WIDTH_A), g_b.reshape(N_HEADS_B, HEAD_DIM))
    return jnp.concatenate([oa.reshape(nseq * T_DEC, WIDTH_A), ob.reshape(nseq * T_DEC, WIDTH_B)], axis=1)


def _cache_shift_kernel(cache_ref, new_ref, out_ref, sem):
    l = pl.program_id(0)
    n_seq, rows = cache_ref.shape[1], cache_ref.shape[2]
    t_new = new_ref.shape[2]

    def copies(b):
        return (pltpu.make_async_copy(cache_ref.at[l, b, pl.ds(t_new, rows - t_new)],
                                      out_ref.at[l, b, pl.ds(0, rows - t_new)], sem.at[0]),
                pltpu.make_async_copy(new_ref.at[l, b], out_ref.at[l, b, pl.ds(rows - t_new, t_new)], sem.at[1]))

    for b in range(n_seq):
        for cp in copies(b):
            cp.start()
    for b in range(n_seq):
        for cp in copies(b):
            cp.wait()


def cache_shift(cache, new_rows):
    any_spec = pl.BlockSpec(memory_space=pl.ANY)
    return pl.pallas_call(
        _cache_shift_kernel,
        grid=(cache.shape[0],),
        in_specs=[any_spec, any_spec],
        out_specs=any_spec,
        out_shape=jax.ShapeDtypeStruct(cache.shape, cache.dtype),
        scratch_shapes=[pltpu.SemaphoreType.DMA((2,))],
        compiler_params=_params(("arbitrary",), 16),
        name="cache_shift",
    )(cache, new_rows)


def _swiglu_step(h, wg, wu, wd):
    g = jnp.dot(h, wg.astype(BF16), preferred_element_type=F32)
    u = jnp.dot(h, wu.astype(BF16), preferred_element_type=F32)
    a = (g * jax.nn.sigmoid(g) * u).astype(BF16)
    return jnp.dot(a, wd.astype(BF16), preferred_element_type=F32)


def _dense_ffn_kernel(h_ref, wg_ref, wu_ref, wd_ref, x_ref, o_ref):
    f = pl.program_id(1)
    y = _swiglu_step(h_ref[...], wg_ref[...], wu_ref[...], wd_ref[...])

    @pl.when(f == 0)
    def _():
        o_ref[...] = x_ref[...] + y

    @pl.when(f > 0)
    def _():
        o_ref[...] += y


def dense_ffn(h, x, w_gate, w_up, w_down, idx, tm, tf):
    t, d = h.shape
    d_ff = w_gate.shape[2]
    return pl.pallas_call(
        _dense_ffn_kernel,
        grid=(t // tm, d_ff // tf),
        in_specs=[pl.BlockSpec((tm, d), lambda i, f: (i, 0)),
                  pl.BlockSpec((None, d, tf), lambda i, f: (idx, 0, f)),
                  pl.BlockSpec((None, d, tf), lambda i, f: (idx, 0, f)),
                  pl.BlockSpec((None, tf, d), lambda i, f: (idx, f, 0)),
                  pl.BlockSpec((tm, d), lambda i, f: (i, 0))],
        out_specs=pl.BlockSpec((tm, d), lambda i, f: (i, 0)),
        out_shape=jax.ShapeDtypeStruct((t, d), F32),
        compiler_params=_params(("arbitrary", "arbitrary"), 56),
        name="dense_ffn",
    )(h, w_gate, w_up, w_down, x)


ROUTE_LANES = 128


def _router_kernel(x_ref, g_ref, wr_ref, h_ref, route_ref):
    h = _rms(x_ref[...], g_ref[...])
    h_ref[...] = h
    logits = jnp.dot(h, wr_ref[...], preferred_element_type=F32, precision=lax.Precision.HIGHEST)
    lane = lax.broadcasted_iota(jnp.int32, logits.shape, 1).astype(F32)
    lg = jnp.where(lane < N_EXPERTS, logits, -jnp.inf)
    m1 = jnp.max(lg, axis=-1, keepdims=True)
    i1 = jnp.min(jnp.where(lg == m1, lane, float(ROUTE_LANES)), axis=-1, keepdims=True)
    lg2 = jnp.where(lane == i1, -jnp.inf, lg)
    m2 = jnp.max(lg2, axis=-1, keepdims=True)
    i2 = jnp.min(jnp.where(lg2 == m2, lane, float(ROUTE_LANES)), axis=-1, keepdims=True)
    e = jnp.exp(m2 - m1)
    g1 = 1.0 / (1.0 + e)
    g2 = e / (1.0 + e)
    route_ref[...] = jnp.where(lane == 0, i1, jnp.where(lane == 1, i2,
                                                        jnp.where(lane == 2, g1, jnp.where(lane == 3, g2, 0.0))))


def router(x, g, w_router, tm):
    t, d = x.shape
    wr = jnp.pad(w_router, ((0, 0), (0, ROUTE_LANES - N_EXPERTS)))
    h, route = pl.pallas_call(
        _router_kernel,
        grid=(t // tm,),
        in_specs=[pl.BlockSpec((tm, d), lambda i: (i, 0)),
                  pl.BlockSpec((1, d), lambda i: (0, 0)),
                  pl.BlockSpec((d, ROUTE_LANES), lambda i: (0, 0))],
        out_specs=[pl.BlockSpec((tm, d), lambda i: (i, 0)),
                   pl.BlockSpec((tm, ROUTE_LANES), lambda i: (i, 0))],
        out_shape=[jax.ShapeDtypeStruct((t, d), F32), jax.ShapeDtypeStruct((t, ROUTE_LANES), F32)],
        compiler_params=_params(("arbitrary",), 40),
        name="router",
    )(x, g.reshape(1, d), wr)
    top_i = route[:, :TOP_K].astype(jnp.int32)
    gates = route[:, TOP_K:2 * TOP_K]
    return h, top_i, gates


def _row_copy(src_ref, dst_ref, src_row, dst_row, sem):
    return pltpu.make_async_copy(src_ref.at[pl.ds(src_row, 1)], dst_ref.at[pl.ds(dst_row, 1)], sem)


def _gather_kernel(idx_ref, src_ref, o_ref, sem):
    rows = o_ref.shape[0]
    base = pl.program_id(0) * rows

    def start(r, carry):
        _row_copy(src_ref, o_ref, idx_ref[base + r], r, sem).start()
        return carry

    def wait(r, carry):
        _row_copy(src_ref, o_ref, idx_ref[base + r], r, sem).wait()
        return carry

    lax.fori_loop(0, rows, start, 0)
    lax.fori_loop(0, rows, wait, 0)


def gather_rows(src, idx, rows_per_step):
    n = idx.shape[0]
    d = src.shape[1]
    return pl.pallas_call(
        _gather_kernel,
        grid_spec=pltpu.PrefetchScalarGridSpec(
            num_scalar_prefetch=1,
            grid=(n // rows_per_step,),
            in_specs=[pl.BlockSpec(memory_space=pl.ANY)],
            out_specs=pl.BlockSpec((rows_per_step, d), lambda i, idx_ref: (i, 0)),
            scratch_shapes=[pltpu.SemaphoreType.DMA(())]),
        out_shape=jax.ShapeDtypeStruct((n, d), src.dtype),
        compiler_params=_params(("arbitrary",), 32),
        name="gather_rows",
    )(idx, src)


def _moe_kernel(te_ref, tv_ref, xs_ref, wg_ref, wu_ref, wd_ref, o_ref, xb_ref):
    t = pl.program_id(0)
    f = pl.program_id(1)
    live = tv_ref[t] > 0

    @pl.when(live & (f == 0))
    def _():
        xb_ref[...] = xs_ref[...].astype(BF16)

    @pl.when(jnp.logical_not(live) & (f == 0))
    def _():
        o_ref[...] = jnp.zeros_like(o_ref)

    @pl.when(live)
    def _():
        y = _swiglu_step(xb_ref[...], wg_ref[...], wu_ref[...], wd_ref[...])

        @pl.when(f == 0)
        def _():
            o_ref[...] = y

        @pl.when(f > 0)
        def _():
            o_ref[...] += y


def moe_experts(xs, tile_expert, tile_live, w_gate, w_up, w_down, idx, tm, tf):
    p, d = xs.shape
    d_ff = w_gate.shape[3]
    nf = d_ff // tf

    def f_of(t, f, tv):
        return jnp.where(tv[t] > 0, f, nf - 1)

    return pl.pallas_call(
        _moe_kernel,
        grid_spec=pltpu.PrefetchScalarGridSpec(
            num_scalar_prefetch=2,
            grid=(p // tm, nf),
            in_specs=[pl.BlockSpec((tm, d), lambda t, f, te, tv: (t, 0)),
                      pl.BlockSpec((None, None, d, tf), lambda t, f, te, tv: (idx, te[t], 0, f_of(t, f, tv))),
                      pl.BlockSpec((None, None, d, tf), lambda t, f, te, tv: (idx, te[t], 0, f_of(t, f, tv))),
                      pl.BlockSpec((None, None, tf, d), lambda t, f, te, tv: (idx, te[t], f_of(t, f, tv), 0))],
            out_specs=pl.BlockSpec((tm, d), lambda t, f, te, tv: (t, 0)),
            scratch_shapes=[pltpu.VMEM((tm, d), BF16)]),
        out_shape=jax.ShapeDtypeStruct((p, d), F32),
        compiler_params=_params(("arbitrary", "arbitrary"), 56),
        name="moe_experts",
    )(tile_expert, tile_live, xs, w_gate, w_up, w_down)


def _combine_kernel(pos_ref, x_ref, gate_ref, ys_ref, o_ref, buf_ref, sem):
    rows = x_ref.shape[0]
    base = pl.program_id(0) * rows

    def copies(r):
        return [_row_copy(ys_ref, buf_ref.at[k], pos_ref[TOP_K * (base + r) + k], r, sem) for k in range(TOP_K)]

    def start(r, carry):
        for cp in copies(r):
            cp.start()
        return carry

    def wait(r, carry):
        for cp in copies(r):
            cp.wait()
        return carry

    lax.fori_loop(0, rows, start, 0)
    lax.fori_loop(0, rows, wait, 0)
    gates = gate_ref[...]
    y = x_ref[...]
    for k in range(TOP_K):
        y = y + gates[:, k:k + 1] * buf_ref[k]
    o_ref[...] = y


def combine_rows(x, gates, ys, pos, rows_per_step):
    t, d = x.shape
    return pl.pallas_call(
        _combine_kernel,
        grid_spec=pltpu.PrefetchScalarGridSpec(
            num_scalar_prefetch=1,
            grid=(t // rows_per_step,),
            in_specs=[pl.BlockSpec((rows_per_step, d), lambda i, pos_ref: (i, 0)),
                      pl.BlockSpec((rows_per_step, TOP_K), lambda i, pos_ref: (i, 0)),
                      pl.BlockSpec(memory_space=pl.ANY)],
            out_specs=pl.BlockSpec((rows_per_step, d), lambda i, pos_ref: (i, 0)),
            scratch_shapes=[pltpu.VMEM((TOP_K, rows_per_step, d), F32), pltpu.SemaphoreType.DMA(())]),
        out_shape=jax.ShapeDtypeStruct((t, d), F32),
        compiler_params=_params(("arbitrary",), 32),
        name="combine_rows",
    )(pos, x, gates, ys)


def moe_ffn(x, g, w_router, w_gate, w_up, w_down, idx, tm, tf):
    t = x.shape[0]
    h, top_i, gates = router(x, g, w_router, 640)
    ids = top_i.reshape(-1)
    onehot = (ids[:, None] == jnp.arange(N_EXPERTS)[None, :]).astype(jnp.int32)
    counts = jnp.sum(onehot, axis=0)
    rank = jnp.take_along_axis(jnp.cumsum(onehot, axis=0) - onehot, ids[:, None], axis=1)[:, 0]
    tiles = (counts + tm - 1) // tm
    tile_end = jnp.cumsum(tiles)
    group_start = (tile_end - tiles) * tm
    pos = group_start[ids] + rank
    n_tiles = (TOP_K * t) // tm + N_EXPERTS
    slot_row = jnp.zeros((n_tiles * tm,), jnp.int32).at[pos].set(jnp.arange(TOP_K * t, dtype=jnp.int32) // TOP_K)
    tile_idx = jnp.arange(n_tiles, dtype=jnp.int32)
    tile_live = (tile_idx < tile_end[-1]).astype(jnp.int32)
    tile_expert = jnp.minimum(jnp.sum((tile_idx[:, None] >= tile_end[None, :]).astype(jnp.int32), axis=1),
                              N_EXPERTS - 1)
    last_live = jnp.take(tile_expert, jnp.maximum(tile_end[-1] - 1, 0))
    tile_expert = jnp.where(tile_live > 0, tile_expert, last_live).astype(jnp.int32)
    xs = gather_rows(h, slot_row, 256)
    ys = moe_experts(xs, tile_expert, tile_live, w_gate, w_up, w_down, idx, tm, tf)
    return combine_rows(x, gates, ys, pos.astype(jnp.int32), 128)


BRANCH_HEADS = {1: 4, 4: 1, 16: 1}


def kernel(x_prompt, x_sample, cache_a_k, cache_a_v, cache_b_k, cache_b_v, g_mix, w_in, sink_a, g_out_a, g_out_b,
           w_out, g_ffn, w_gate_dense, w_up_dense, w_down_dense, w_router, w_gate_moe, w_up_moe, w_down_moe,
           g_final):
    n_batch, seq, d = x_prompt.shape
    n_seq, t_dec, _ = x_sample.shape
    depth = w_in.shape[0]
    tp = n_batch * seq
    ts = n_seq * t_dec
    t_all = tp + ts
    tm = 640
    assert t_all % tm == 0 and t_dec == T_DEC and tp % BLOCK == 0 and ts == BLOCK
    assert cache_a_k.shape[2] == WINDOW_A and cache_b_k.shape[2] == WINDOW_B_MAX
    keep_a, keep_b = min(WINDOW_A, seq), min(WINDOW_B_MAX, seq)
    x = jnp.concatenate([x_prompt.reshape(tp, d), x_sample.reshape(ts, d)], axis=0)
    prompt_kv = [[] for _ in range(4)]
    sample_kv = [[] for _ in range(4)]
    for l in range(depth):
        h = rmsnorm_rows(x, g_mix[l], BF16, tm, 0, t_all // tm)
        proj = matmul_layer(h, w_in, l, tm, 768)
        proj_s = proj[tp:]
        oa = band_attention_a(proj, sink_a[l], n_batch, seq)
        obs, lses = zip(*[band_attention_b(proj, n_batch, seq, r, BRANCH_HEADS[r]) for _, r in DILATED_BRANCHES])
        o_p = merge_prompt(oa, obs, lses, g_out_a[l], g_out_b[l], 256)
        o_s = sample_attention(proj_s, l, sink_a[l], cache_a_k, cache_a_v, cache_b_k, cache_b_v,
                               g_out_a[l], g_out_b[l])
        o_all = jnp.concatenate([o_p, o_s.astype(BF16)], axis=0)
        x = matmul_layer(o_all, w_out, l, tm, 512, residual=x)
        if l % 2 == 0:
            h2 = rmsnorm_rows(x, g_ffn[l], BF16, tm, 0, t_all // tm)
            x = dense_ffn(h2, x, w_gate_dense, w_up_dense, w_down_dense, l // 2, tm, 256)
        else:
            x = moe_ffn(x, g_ffn[l], w_router[l // 2], w_gate_moe, w_up_moe, w_down_moe, l // 2, 512, 512)
        for n, (c0, heads, keep) in enumerate(((COL_KA, KV_HEADS_A, keep_a), (COL_VA, KV_HEADS_A, keep_a),
                                               (COL_KB, N_HEADS_B, keep_b), (COL_VB, N_HEADS_B, keep_b))):
            c1 = c0 + heads * HEAD_DIM
            prompt_kv[n].append(jnp.stack([proj[(b + 1) * seq - keep:(b + 1) * seq, c0:c1]
                                           for b in range(n_batch)]).reshape(n_batch, keep, heads, HEAD_DIM))
            sample_kv[n].append(proj_s[:, c0:c1].reshape(n_seq, t_dec, heads, HEAD_DIM))
    y_prompt = rmsnorm_rows(x, g_final, F32, 512, 0, tp // 512).reshape(n_batch, seq, d)
    y_sample = rmsnorm_rows(x, g_final, F32, ts, tp // ts, 1).reshape(n_seq, t_dec, d)
    new_sample = [cache_shift(c, jnp.stack(rows))
                  for c, rows in zip((cache_a_k, cache_a_v, cache_b_k, cache_b_v), sample_kv)]
    return (y_prompt, y_sample) + tuple(jnp.stack(p) for p in prompt_kv) + tuple(new_sample)
```

```python
import functools

import jax
import jax.numpy as jnp
import numpy as np
from jax import lax
from jax.experimental import pallas as pl
from jax.experimental.pallas import tpu as pltpu

F32 = jnp.float32
BF16 = jnp.bfloat16

D_MODEL = 2048
HEAD_DIM = 128
N_HEADS_A = 8
KV_HEADS_A = 2
GROUP_A = 4
N_HEADS_B = 8
WIDTH_A = N_HEADS_A * HEAD_DIM
KV_WIDTH_A = KV_HEADS_A * HEAD_DIM
WIDTH_B = N_HEADS_B * HEAD_DIM
PROJ_WIDTH = WIDTH_A + 2 * KV_WIDTH_A + 3 * WIDTH_B
COL_KA = WIDTH_A
COL_VA = COL_KA + KV_WIDTH_A
COL_QB = COL_VA + KV_WIDTH_A
COL_KB = COL_QB + WIDTH_B
COL_VB = COL_KB + WIDTH_B
WINDOW_A = 128
DILATED_BRANCHES = ((128, 1), (512, 4), (2048, 16))
WINDOW_B_MAX = 2048
BLOCK = 128
N_EXPERTS = 8
TOP_K = 2
EPS = 1e-5
NEG_INF = -1e30
SCALE = HEAD_DIM ** -0.5
SLOPES = tuple(2.0 ** (-8.0 * i / (N_HEADS_A + N_HEADS_B)) for i in range(1, N_HEADS_A + N_HEADS_B + 1))
SLOPES_A = SLOPES[:N_HEADS_A]
SLOPES_B = SLOPES[N_HEADS_A:]

MIB = 1024 * 1024


def _params(semantics, vmem_mib):
    return pltpu.CompilerParams(dimension_semantics=semantics, vmem_limit_bytes=vmem_mib * MIB)


def _bdot(a, b):
    return jnp.dot(a.astype(BF16), b.astype(BF16), preferred_element_type=F32)


def _bdot_nt(a, b):
    return lax.dot_general(a.astype(BF16), b.astype(BF16), (((1,), (1,)), ((), ())),
                           preferred_element_type=F32)


def _rms(x, g):
    return x * lax.rsqrt(jnp.mean(x * x, axis=-1, keepdims=True) + EPS) * g


def _rmsnorm_kernel(x_ref, g_ref, o_ref):
    o_ref[...] = _rms(x_ref[...], g_ref[...]).astype(o_ref.dtype)


def rmsnorm_rows(x, g, out_dtype, row_block, first_block, n_blocks):
    d = x.shape[1]
    return pl.pallas_call(
        _rmsnorm_kernel,
        grid=(n_blocks,),
        in_specs=[pl.BlockSpec((row_block, d), lambda i: (i + first_block, 0)),
                  pl.BlockSpec((1, d), lambda i: (0, 0))],
        out_specs=pl.BlockSpec((row_block, d), lambda i: (i, 0)),
        out_shape=jax.ShapeDtypeStruct((n_blocks * row_block, d), out_dtype),
        compiler_params=_params(("arbitrary",), 40),
        name="rmsnorm",
    )(x, g.reshape(1, d))


def _matmul_kernel(x_ref, w_ref, o_ref, wb_ref):
    @pl.when(pl.program_id(1) == 0)
    def _():
        wb_ref[...] = w_ref[...].astype(BF16)

    o_ref[...] = jnp.dot(x_ref[...], wb_ref[...], preferred_element_type=F32)


def _matmul_res_kernel(x_ref, w_ref, r_ref, o_ref, wb_ref):
    @pl.when(pl.program_id(1) == 0)
    def _():
        wb_ref[...] = w_ref[...].astype(BF16)

    o_ref[...] = r_ref[...] + jnp.dot(x_ref[...], wb_ref[...], preferred_element_type=F32)


def matmul_layer(x, w_all, layer, tm, tn, residual=None):
    t, k = x.shape
    n = w_all.shape[2]
    grid = (n // tn, t // tm)
    in_specs = [pl.BlockSpec((tm, k), lambda j, i: (i, 0)),
                pl.BlockSpec((None, k, tn), lambda j, i: (layer, 0, j))]
    args = [x, w_all]
    kern = _matmul_kernel
    if residual is not None:
        in_specs.append(pl.BlockSpec((tm, tn), lambda j, i: (i, j)))
        args.append(residual)
        kern = _matmul_res_kernel
    return pl.pallas_call(
        kern,
        grid=grid,
        in_specs=in_specs,
        out_specs=pl.BlockSpec((tm, tn), lambda j, i: (i, j)),
        out_shape=jax.ShapeDtypeStruct((t, n), F32),
        scratch_shapes=[pltpu.VMEM((k, tn), BF16)],
        compiler_params=_params(("arbitrary", "arbitrary"), 56),
        name="matmul",
    )(*args)


def _band_mask(has_prev):
    i = lax.broadcasted_iota(jnp.int32, (BLOCK, 2 * BLOCK), 0)
    j = lax.broadcasted_iota(jnp.int32, (BLOCK, 2 * BLOCK), 1)
    dist = BLOCK + i - j
    valid = (dist >= 0) & (dist <= BLOCK) & ((j >= BLOCK) | has_prev)
    return dist.astype(F32), valid


def _band_head(q, kp, kc, vp, vc, slope, dist_f, valid, sink):
    k = jnp.concatenate([kp, kc], axis=0)
    v = jnp.concatenate([vp, vc], axis=0)
    s = _bdot_nt(q, k) * SCALE - slope * dist_f
    s = jnp.where(valid, s, NEG_INF)
    m = jnp.max(s, axis=-1, keepdims=True)
    if sink is not None:
        m = jnp.maximum(m, sink)
    p = jnp.exp(s - m)
    l = jnp.sum(p, axis=-1, keepdims=True)
    if sink is not None:
        l = l + jnp.exp(sink - m)
    return _bdot(p, v) / l, m + jnp.log(l)


def _band_a_kernel(sink_ref, q_ref, kp_ref, kc_ref, vp_ref, vc_ref, o_ref):
    hk = pl.program_id(2)
    dist_f, valid = _band_mask(pl.program_id(1) > 0)
    for g in range(GROUP_A):
        cols = slice(g * HEAD_DIM, (g + 1) * HEAD_DIM)
        slope = jnp.where(hk == 0, SLOPES_A[g], SLOPES_A[GROUP_A + g])
        o, _ = _band_head(q_ref[:, cols], kp_ref[...], kc_ref[...], vp_ref[...], vc_ref[...], slope,
                          dist_f, valid, sink_ref[0, hk * GROUP_A + g])
        o_ref[:, cols] = o


def band_attention_a(proj, sink, n_batch, seq):
    nb = seq // BLOCK
    w = GROUP_A * HEAD_DIM
    ka0, va0 = COL_KA // HEAD_DIM, COL_VA // HEAD_DIM

    def row(b, ib):
        return b * nb + ib

    def prev(b, ib):
        return b * nb + jnp.maximum(ib - 1, 0)

    return pl.pallas_call(
        _band_a_kernel,
        grid=(n_batch, nb, KV_HEADS_A),
        in_specs=[pl.BlockSpec(memory_space=pltpu.SMEM),
                  pl.BlockSpec((BLOCK, w), lambda b, ib, hk: (row(b, ib), hk)),
                  pl.BlockSpec((BLOCK, HEAD_DIM), lambda b, ib, hk: (prev(b, ib), ka0 + hk)),
                  pl.BlockSpec((BLOCK, HEAD_DIM), lambda b, ib, hk: (row(b, ib), ka0 + hk)),
                  pl.BlockSpec((BLOCK, HEAD_DIM), lambda b, ib, hk: (prev(b, ib), va0 + hk)),
                  pl.BlockSpec((BLOCK, HEAD_DIM), lambda b, ib, hk: (row(b, ib), va0 + hk))],
        out_specs=pl.BlockSpec((BLOCK, w), lambda b, ib, hk: (row(b, ib), hk)),
        out_shape=jax.ShapeDtypeStruct((n_batch * seq, WIDTH_A), F32),
        compiler_params=_params(("arbitrary",) * 3, 32),
        name="band_attn_a",
    )(sink.reshape(1, N_HEADS_A), proj, proj, proj, proj, proj)


def _band_b_kernel(slope_ref, q_ref, kp_ref, kc_ref, vp_ref, vc_ref, o_ref, lse_ref, *, dilation, heads):
    hg = pl.program_id(1)
    dist_f, valid = _band_mask(pl.program_id(2) > 0)

    def residue(rows):
        for hh in range(heads):
            cols = slice(hh * HEAD_DIM, (hh + 1) * HEAD_DIM)
            o, lse = _band_head(q_ref[rows, cols], kp_ref[rows, cols], kc_ref[rows, cols], vp_ref[rows, cols],
                                vc_ref[rows, cols], slope_ref[0, hg * heads + hh], dist_f, valid, None)
            o_ref[rows, cols] = o
            lse_ref[rows, cols] = jnp.broadcast_to(lse, (BLOCK, HEAD_DIM))

    if dilation == 1:
        residue(slice(None))
    else:
        def body(res, carry):
            residue(pl.ds(res, BLOCK, stride=dilation))
            return carry

        lax.fori_loop(0, dilation, body, 0)


def band_attention_b(proj, n_batch, seq, dilation, heads):
    r = dilation
    chunk = BLOCK * r
    nc = seq // chunk
    w = heads * HEAD_DIM
    q0, k0, v0 = COL_QB // w, COL_KB // w, COL_VB // w

    def row(b, ic):
        return b * nc + ic

    def prev(b, ic):
        return b * nc + jnp.maximum(ic - 1, 0)

    slopes = jnp.asarray([[s * r for s in SLOPES_B]], F32)
    kern = functools.partial(_band_b_kernel, dilation=r, heads=heads)
    out_sds = jax.ShapeDtypeStruct((n_batch * seq, WIDTH_B), F32)
    out_spec = pl.BlockSpec((chunk, w), lambda b, hg, ic: (row(b, ic), hg))
    return pl.pallas_call(
        kern,
        grid=(n_batch, WIDTH_B // w, nc),
        in_specs=[pl.BlockSpec(memory_space=pltpu.SMEM),
                  pl.BlockSpec((chunk, w), lambda b, hg, ic: (row(b, ic), q0 + hg)),
                  pl.BlockSpec((chunk, w), lambda b, hg, ic: (prev(b, ic), k0 + hg)),
                  pl.BlockSpec((chunk, w), lambda b, hg, ic: (row(b, ic), k0 + hg)),
                  pl.BlockSpec((chunk, w), lambda b, hg, ic: (prev(b, ic), v0 + hg)),
                  pl.BlockSpec((chunk, w), lambda b, hg, ic: (row(b, ic), v0 + hg))],
        out_specs=[out_spec, out_spec],
        out_shape=[out_sds, out_sds],
        compiler_params=_params(("arbitrary",) * 3, 40),
        name="band_attn_b",
    )(slopes, proj, proj, proj, proj, proj)


def _merge_kernel(oa_ref, o1_ref, o2_ref, o3_ref, l1_ref, l2_ref, l3_ref, ga_ref, gb_ref, out_ref):
    l1, l2, l3 = l1_ref[...], l2_ref[...], l3_ref[...]
    m = jnp.maximum(jnp.maximum(l1, l2), l3)
    e1, e2, e3 = jnp.exp(l1 - m), jnp.exp(l2 - m), jnp.exp(l3 - m)
    ob = (e1 * o1_ref[...] + e2 * o2_ref[...] + e3 * o3_ref[...]) / (e1 + e2 + e3)
    out_ref[:, :WIDTH_A] = _rms(oa_ref[...], ga_ref[...]).astype(out_ref.dtype)
    out_ref[:, WIDTH_A:] = _rms(ob, gb_ref[...]).astype(out_ref.dtype)


def merge_prompt(oa, obs, lses, g_a, g_b, row_block):
    t = oa.shape[0]
    spec = pl.BlockSpec((row_block, WIDTH_A), lambda i: (i, 0))
    gspec = pl.BlockSpec((1, WIDTH_A), lambda i: (0, 0))
    return pl.pallas_call(
        _merge_kernel,
        grid=(t // row_block,),
        in_specs=[spec] * 7 + [gspec, gspec],
        out_specs=pl.BlockSpec((row_block, WIDTH_A + WIDTH_B), lambda i: (i, 0)),
        out_shape=jax.ShapeDtypeStruct((t, WIDTH_A + WIDTH_B), BF16),
        compiler_params=_params(("arbitrary",), 40),
        name="merge_prompt",
    )(oa, *obs, *lses, g_a.reshape(1, WIDTH_A), g_b.reshape(1, WIDTH_B))


T_DEC = 4
ROWS_16 = WINDOW_B_MAX // 16
ROWS_4 = 512


def _select_by_index(idx, values):
    out = values[-1]
    for n in range(len(values) - 2, -1, -1):
        out = jnp.where(idx == n, values[n], out)
    return out


def _sample_group_a(sink_ref, qa_ref, kn_ref, cak_ref, cav_ref, ga_ref, oa_ref):
    kn = kn_ref[0]
    rows_a = GROUP_A * T_DEC
    for hk in range(KV_HEADS_A):
        q = qa_ref[0, hk * rows_a:(hk + 1) * rows_a, :]
        row = lax.broadcasted_iota(jnp.int32, (rows_a, 1), 0)
        t_row, g_row = row & (T_DEC - 1), row >> 2
        slope = _select_by_index(g_row, [SLOPES_A[hk * GROUP_A + g] for g in range(GROUP_A)])
        sink = _select_by_index(g_row, [sink_ref[0, hk * GROUP_A + g] for g in range(GROUP_A)])
        kc = cak_ref[0, 0, :, hk * HEAD_DIM:(hk + 1) * HEAD_DIM]
        vc = cav_ref[0, 0, :, hk * HEAD_DIM:(hk + 1) * HEAD_DIM]
        c = lax.broadcasted_iota(jnp.int32, (rows_a, WINDOW_A), 1)
        dist = t_row + WINDOW_A - c
        s_c = _bdot_nt(q, kc) * SCALE - slope * dist.astype(F32)
        s_c = jnp.where(dist <= WINDOW_A, s_c, NEG_INF)
        news = []
        for tp in range(T_DEC):
            k_row = kn[tp:tp + 1, COL_KA + hk * HEAD_DIM:COL_KA + (hk + 1) * HEAD_DIM]
            v_row = kn[tp:tp + 1, COL_VA + hk * HEAD_DIM:COL_VA + (hk + 1) * HEAD_DIM]
            s = jnp.sum(q * k_row, axis=-1, keepdims=True) * SCALE - slope * (t_row - tp).astype(F32)
            news.append((jnp.where(t_row >= tp, s, NEG_INF), v_row))
        m = jnp.maximum(jnp.max(s_c, axis=-1, keepdims=True), sink)
        for s_n, _ in news:
            m = jnp.maximum(m, s_n)
        p_c = jnp.exp(s_c - m)
        l = jnp.sum(p_c, axis=-1, keepdims=True) + jnp.exp(sink - m)
        acc = _bdot(p_c, vc)
        for s_n, v_row in news:
            p_n = jnp.exp(s_n - m)
            l = l + p_n
            acc = acc + p_n * v_row
        o = acc / l
        for g in range(GROUP_A):
            col = (hk * GROUP_A + g) * HEAD_DIM
            oa_ref[0, :, col:col + HEAD_DIM] = o[g * T_DEC:(g + 1) * T_DEC, :]
    oa_ref[0] = _rms(oa_ref[0], ga_ref[...])


def _sample_group_b(qb_ref, knb_ref, vnb_ref, ck16_ref, cv16_ref, ck4_ref, cv4_ref, gb_ref, ob_ref):
    head = lax.broadcasted_iota(jnp.int32, (1, N_HEADS_B, 1), 1)
    slope = _select_by_index(head, list(SLOPES_B))
    i3 = lax.broadcasted_iota(jnp.int32, (ROWS_16, 1, 1), 0)
    i3f = i3.astype(F32)
    for t in range(T_DEC):
        q = qb_ref[0, t] * SCALE

        def scores(k3, dist3):
            return jnp.sum(k3 * q[None], axis=-1, keepdims=True) - slope * dist3

        s16 = scores(ck16_ref[0, 0, :, t], WINDOW_B_MAX - 16.0 * i3f)
        s4 = scores(ck4_ref[0, 0, pl.ds(t, ROWS_16, stride=4)], ROWS_4 - 4.0 * i3f)
        s1 = scores(ck4_ref[0, 0, ROWS_4 - BLOCK:ROWS_4], BLOCK + t - i3f)
        s1 = jnp.where(i3 >= t, s1, NEG_INF)
        news = []
        for tp in range(t + 1):
            s = jnp.sum(knb_ref[0, tp] * q, axis=-1, keepdims=True) - slope[0] * float(t - tp)
            news.append((s, 3.0 if tp == t else 1.0, vnb_ref[0, tp]))
        m = jnp.maximum(jnp.maximum(jnp.max(s16, axis=0), jnp.max(s4, axis=0)), jnp.max(s1, axis=0))
        for s_n, _, _ in news:
            m = jnp.maximum(m, s_n)
        p16, p4, p1 = jnp.exp(s16 - m[None]), jnp.exp(s4 - m[None]), jnp.exp(s1 - m[None])
        l = jnp.sum(p16, axis=0) + jnp.sum(p4, axis=0) + jnp.sum(p1, axis=0)
        acc = (jnp.sum(p16 * cv16_ref[0, 0, :, t], axis=0)
               + jnp.sum(p4 * cv4_ref[0, 0, pl.ds(t, ROWS_16, stride=4)], axis=0)
               + jnp.sum(p1 * cv4_ref[0, 0, ROWS_4 - BLOCK:ROWS_4], axis=0))
        for s_n, wgt, v_new in news:
            p_n = wgt * jnp.exp(s_n - m)
            l = l + p_n
            acc = acc + p_n * v_new
        o = acc / l
        ms = jnp.sum(jnp.sum(o * o, axis=1, keepdims=True), axis=0, keepdims=True) / WIDTH_B
        ob_ref[0, t] = o * lax.rsqrt(ms + EPS) * gb_ref[...]


def _sample_attn_kernel(sink_ref, qa_ref, kn_ref, cak_ref, cav_ref, qb_ref, knb_ref, vnb_ref, ck16_ref, cv16_ref,
                        ck4_ref, cv4_ref, ga_ref, gb_ref, oa_ref, ob_ref):
    _sample_group_a(sink_ref, qa_ref, kn_ref, cak_ref, cav_ref, ga_ref, oa_ref)
    _sample_group_b(qb_ref, knb_ref, vnb_ref, ck16_ref, cv16_ref, ck4_ref, cv4_ref, gb_ref, ob_ref)


def sample_attention(proj_s, layer, sink, cache_a_k, cache_a_v, cache_b_k, cache_b_v, g_a, g_b):
    nseq = proj_s.shape[0] // T_DEC
    depth = cache_a_k.shape[0]
    p3 = proj_s.reshape(nseq, T_DEC, PROJ_WIDTH)
    qa = p3[:, :, :WIDTH_A].reshape(nseq, T_DEC, N_HEADS_A, HEAD_DIM).transpose(0, 2, 1, 3)
    qa = qa.reshape(nseq, N_HEADS_A * T_DEC, HEAD_DIM)
    qb = p3[:, :, COL_QB:COL_KB].reshape(nseq, T_DEC, N_HEADS_B, HEAD_DIM)
    knb = p3[:, :, COL_KB:COL_VB].reshape(nseq, T_DEC, N_HEADS_B, HEAD_DIM)
    vnb = p3[:, :, COL_VB:].reshape(nseq, T_DEC, N_HEADS_B, HEAD_DIM)
    ca_k = cache_a_k.reshape(depth, nseq, WINDOW_A, KV_WIDTH_A)
    ca_v = cache_a_v.reshape(depth, nseq, WINDOW_A, KV_WIDTH_A)
    cb16_k = cache_b_k.reshape(depth, nseq, ROWS_16, 16, N_HEADS_B, HEAD_DIM)
    cb16_v = cache_b_v.reshape(depth, nseq, ROWS_16, 16, N_HEADS_B, HEAD_DIM)
    last4 = WINDOW_B_MAX // ROWS_4 - 1
    a_spec = pl.BlockSpec((1, 1, WINDOW_A, KV_WIDTH_A), lambda b: (layer, b, 0, 0))
    b16_spec = pl.BlockSpec((1, 1, ROWS_16, T_DEC, N_HEADS_B, HEAD_DIM), lambda b: (layer, b, 0, 0, 0, 0))
    b4_spec = pl.BlockSpec((1, 1, ROWS_4, N_HEADS_B, HEAD_DIM), lambda b: (layer, b, last4, 0, 0))
    tok_spec = pl.BlockSpec((1, T_DEC, N_HEADS_B, HEAD_DIM), lambda b: (b, 0, 0, 0))
    oa, ob = pl.pallas_call(
        _sample_attn_kernel,
        grid=(nseq,),
        in_specs=[pl.BlockSpec(memory_space=pltpu.SMEM),
                  pl.BlockSpec((1, N_HEADS_A * T_DEC, HEAD_DIM), lambda b: (b, 0, 0)),
                  pl.BlockSpec((1, T_DEC, PROJ_WIDTH), lambda b: (b, 0, 0)),
                  a_spec, a_spec, tok_spec, tok_spec, tok_spec, b16_spec, b16_spec, b4_spec, b4_spec,
                  pl.BlockSpec((1, WIDTH_A), lambda b: (0, 0)),
                  pl.BlockSpec((N_HEADS_B, HEAD_DIM), lambda b: (0, 0))],
        out_specs=[pl.BlockSpec((1, T_DEC, WIDTH_A), lambda b: (b, 0, 0)), tok_spec],
        out_shape=[jax.ShapeDtypeStruct((nseq, T_DEC, WIDTH_A), F32),
                   jax.ShapeDtypeStruct((nseq, T_DEC, N_HEADS_B, HEAD_DIM), F32)],
        compiler_params=_params(("arbitrary",), 48),
        name="sample_attn",
    )(sink.reshape(1, N_HEADS_A), qa, p3, ca_k, ca_v, qb, knb, vnb, cb16_k, cb16_v, cache_b_k, cache_b_v,
      g_a.reshape(1, WIDTH_A), g_b.reshape(N_HEADS_B, HEAD_DIM))
    return jnp.concatenate([oa.reshape(nseq * T_DEC, WIDTH_A), ob.reshape(nseq * T_DEC, WIDTH_B)], axis=1)


CACHE_BLOCK_BYTES = 8 * MIB


def _cache_shift_kernel(cache_ref, new_ref, out_ref):
    rows, t_new = cache_ref.shape[2], new_ref.shape[2]
    out_ref[:, :, :rows - t_new] = cache_ref[:, :, t_new:]
    out_ref[:, :, rows - t_new:] = new_ref[...]


def cache_shift(cache, new_rows):
    depth, n_seq, rows, heads, hd = cache.shape
    t_new = new_rows.shape[2]
    seq_bytes = rows * max(heads, 8) * hd * cache.dtype.itemsize
    nb = max(1, min(n_seq, CACHE_BLOCK_BYTES // seq_bytes))
    assert n_seq % nb == 0
    return pl.pallas_call(
        _cache_shift_kernel,
        grid=(depth, n_seq // nb),
        in_specs=[pl.BlockSpec((1, nb, rows, heads, hd), lambda l, b: (l, b, 0, 0, 0)),
                  pl.BlockSpec((1, nb, t_new, heads, hd), lambda l, b: (l, b, 0, 0, 0))],
        out_specs=pl.BlockSpec((1, nb, rows, heads, hd), lambda l, b: (l, b, 0, 0, 0)),
        out_shape=jax.ShapeDtypeStruct(cache.shape, cache.dtype),
        compiler_params=_params(("arbitrary", "arbitrary"), 48),
        name="cache_shift",
    )(cache, new_rows)


def _group_starts(te_ref, t):
    return (t == 0) | (te_ref[t] != te_ref[jnp.maximum(t - 1, 0)])


def _gate_up_kernel(te_ref, tv_ref, x_ref, wg_ref, wu_ref, a_ref, wgb_ref, wub_ref):
    t = pl.program_id(1)
    live = tv_ref[t] > 0

    @pl.when(live & _group_starts(te_ref, t))
    def _():
        wgb_ref[...] = wg_ref[...].astype(BF16)
        wub_ref[...] = wu_ref[...].astype(BF16)

    @pl.when(live)
    def _():
        x = x_ref[...]
        g = jnp.dot(x, wgb_ref[...], preferred_element_type=F32)
        u = jnp.dot(x, wub_ref[...], preferred_element_type=F32)
        a_ref[...] = (g * jax.nn.sigmoid(g) * u).astype(BF16)

    @pl.when(jnp.logical_not(live))
    def _():
        a_ref[...] = jnp.zeros_like(a_ref)


def _down_kernel(te_ref, tv_ref, a_ref, wd_ref, *rest):
    r_ref = rest[0] if len(rest) == 3 else None
    o_ref, wdb_ref = rest[-2:]
    t = pl.program_id(1)
    live = tv_ref[t] > 0

    @pl.when(live & _group_starts(te_ref, t))
    def _():
        wdb_ref[...] = wd_ref[...].astype(BF16)

    @pl.when(live)
    def _():
        y = jnp.dot(a_ref[...], wdb_ref[...], preferred_element_type=F32)
        o_ref[...] = y if r_ref is None else r_ref[...] + y

    @pl.when(jnp.logical_not(live))
    def _():
        o_ref[...] = jnp.zeros_like(o_ref) if r_ref is None else r_ref[...]


def grouped_swiglu(x, tile_group, tile_live, w_gate, w_up, w_down, idx, tm, tf, tn, residual=None):
    p, d = x.shape
    d_ff = w_gate.shape[3]
    n_tiles = p // tm
    act = pl.pallas_call(
        _gate_up_kernel,
        grid_spec=pltpu.PrefetchScalarGridSpec(
            num_scalar_prefetch=2,
            grid=(d_ff // tf, n_tiles),
            in_specs=[pl.BlockSpec((tm, d), lambda f, t, te, tv: (t, 0)),
                      pl.BlockSpec((None, None, d, tf), lambda f, t, te, tv: (idx, te[t], 0, f)),
                      pl.BlockSpec((None, None, d, tf), lambda f, t, te, tv: (idx, te[t], 0, f))],
            out_specs=pl.BlockSpec((tm, tf), lambda f, t, te, tv: (t, f)),
            scratch_shapes=[pltpu.VMEM((d, tf), BF16), pltpu.VMEM((d, tf), BF16)]),
        out_shape=jax.ShapeDtypeStruct((p, d_ff), BF16),
        compiler_params=_params(("arbitrary", "arbitrary"), 60),
        name="swiglu_gate_up",
    )(tile_group, tile_live, x, w_gate, w_up)
    in_specs = [pl.BlockSpec((tm, d_ff), lambda n, t, te, tv: (t, 0)),
                pl.BlockSpec((None, None, d_ff, tn), lambda n, t, te, tv: (idx, te[t], 0, n))]
    args = [act, w_down]
    if residual is not None:
        in_specs.append(pl.BlockSpec((tm, tn), lambda n, t, te, tv: (t, n)))
        args.append(residual)
    return pl.pallas_call(
        _down_kernel,
        grid_spec=pltpu.PrefetchScalarGridSpec(
            num_scalar_prefetch=2,
            grid=(d // tn, n_tiles),
            in_specs=in_specs,
            out_specs=pl.BlockSpec((tm, tn), lambda n, t, te, tv: (t, n)),
            scratch_shapes=[pltpu.VMEM((d_ff, tn), BF16)]),
        out_shape=jax.ShapeDtypeStruct((p, d), F32),
        compiler_params=_params(("arbitrary", "arbitrary"), 60),
        name="swiglu_down",
    )(tile_group, tile_live, *args)


def dense_ffn(h, x, w_gate, w_up, w_down, idx, tm, tf, tn):
    n_tiles = h.shape[0] // tm
    one_group = jnp.zeros((n_tiles,), jnp.int32)
    all_live = jnp.ones((n_tiles,), jnp.int32)
    return grouped_swiglu(h, one_group, all_live, w_gate[:, None], w_up[:, None], w_down[:, None], idx,
                          tm, tf, tn, residual=x)


ROUTE_LANES = 128


def _router_kernel(x_ref, g_ref, wr_ref, h_ref, route_ref):
    h = _rms(x_ref[...], g_ref[...])
    h_ref[...] = h
    logits = jnp.dot(h, wr_ref[...], preferred_element_type=F32, precision=lax.Precision.HIGHEST)
    lane = lax.broadcasted_iota(jnp.int32, logits.shape, 1).astype(F32)
    lg = jnp.where(lane < N_EXPERTS, logits, -jnp.inf)
    m1 = jnp.max(lg, axis=-1, keepdims=True)
    i1 = jnp.min(jnp.where(lg == m1, lane, float(ROUTE_LANES)), axis=-1, keepdims=True)
    lg2 = jnp.where(lane == i1, -jnp.inf, lg)
    m2 = jnp.max(lg2, axis=-1, keepdims=True)
    i2 = jnp.min(jnp.where(lg2 == m2, lane, float(ROUTE_LANES)), axis=-1, keepdims=True)
    e = jnp.exp(m2 - m1)
    g1 = 1.0 / (1.0 + e)
    g2 = e / (1.0 + e)
    route_ref[...] = jnp.where(lane == 0, i1, jnp.where(lane == 1, i2,
                                                        jnp.where(lane == 2, g1, jnp.where(lane == 3, g2, 0.0))))


def router(x, g, w_router, tm):
    t, d = x.shape
    wr = jnp.pad(w_router, ((0, 0), (0, ROUTE_LANES - N_EXPERTS)))
    h, route = pl.pallas_call(
        _router_kernel,
        grid=(t // tm,),
        in_specs=[pl.BlockSpec((tm, d), lambda i: (i, 0)),
                  pl.BlockSpec((1, d), lambda i: (0, 0)),
                  pl.BlockSpec((d, ROUTE_LANES), lambda i: (0, 0))],
        out_specs=[pl.BlockSpec((tm, d), lambda i: (i, 0)),
                   pl.BlockSpec((tm, ROUTE_LANES), lambda i: (i, 0))],
        out_shape=[jax.ShapeDtypeStruct((t, d), F32), jax.ShapeDtypeStruct((t, ROUTE_LANES), F32)],
        compiler_params=_params(("arbitrary",), 40),
        name="router",
    )(x, g.reshape(1, d), wr)
    top_i = route[:, :TOP_K].astype(jnp.int32)
    gates = route[:, TOP_K:2 * TOP_K]
    return h, top_i, gates


def _row_copy(src_ref, dst_ref, src_row, dst_row, sem):
    return pltpu.make_async_copy(src_ref.at[pl.ds(src_row, 1)], dst_ref.at[pl.ds(dst_row, 1)], sem)


def _gather_kernel(idx_ref, src_ref, o_ref, buf_ref, sem):
    rows = o_ref.shape[0]
    base = pl.program_id(0) * rows

    def start(r, carry):
        _row_copy(src_ref, buf_ref, idx_ref[base + r], r, sem).start()
        return carry

    def wait(r, carry):
        _row_copy(src_ref, buf_ref, idx_ref[base + r], r, sem).wait()
        return carry

    lax.fori_loop(0, rows, start, 0)
    lax.fori_loop(0, rows, wait, 0)
    o_ref[...] = buf_ref[...].astype(o_ref.dtype)


def gather_rows(src, idx, rows_per_step, out_dtype):
    n = idx.shape[0]
    d = src.shape[1]
    return pl.pallas_call(
        _gather_kernel,
        grid_spec=pltpu.PrefetchScalarGridSpec(
            num_scalar_prefetch=1,
            grid=(n // rows_per_step,),
            in_specs=[pl.BlockSpec(memory_space=pl.ANY)],
            out_specs=pl.BlockSpec((rows_per_step, d), lambda i, idx_ref: (i, 0)),
            scratch_shapes=[pltpu.VMEM((rows_per_step, d), src.dtype), pltpu.SemaphoreType.DMA(())]),
        out_shape=jax.ShapeDtypeStruct((n, d), out_dtype),
        compiler_params=_params(("arbitrary",), 32),
        name="gather_rows",
    )(idx, src)


def _combine_kernel(pos_ref, x_ref, gate_ref, ys_ref, o_ref, buf_ref, sem):
    rows = x_ref.shape[0]
    base = pl.program_id(0) * rows

    def copies(r):
        return [_row_copy(ys_ref, buf_ref.at[k], pos_ref[TOP_K * (base + r) + k], r, sem) for k in range(TOP_K)]

    def start(r, carry):
        for cp in copies(r):
            cp.start()
        return carry

    def wait(r, carry):
        for cp in copies(r):
            cp.wait()
        return carry

    lax.fori_loop(0, rows, start, 0)
    lax.fori_loop(0, rows, wait, 0)
    gates = gate_ref[...]
    y = x_ref[...]
    for k in range(TOP_K):
        y = y + gates[:, k:k + 1] * buf_ref[k]
    o_ref[...] = y


def combine_rows(x, gates, ys, pos, rows_per_step):
    t, d = x.shape
    return pl.pallas_call(
        _combine_kernel,
        grid_spec=pltpu.PrefetchScalarGridSpec(
            num_scalar_prefetch=1,
            grid=(t // rows_per_step,),
            in_specs=[pl.BlockSpec((rows_per_step, d), lambda i, pos_ref: (i, 0)),
                      pl.BlockSpec((rows_per_step, TOP_K), lambda i, pos_ref: (i, 0)),
                      pl.BlockSpec(memory_space=pl.ANY)],
            out_specs=pl.BlockSpec((rows_per_step, d), lambda i, pos_ref: (i, 0)),
            scratch_shapes=[pltpu.VMEM((TOP_K, rows_per_step, d), F32), pltpu.SemaphoreType.DMA(())]),
        out_shape=jax.ShapeDtypeStruct((t, d), F32),
        compiler_params=_params(("arbitrary",), 32),
        name="combine_rows",
    )(pos, x, gates, ys)


def moe_ffn(x, g, w_router, w_gate, w_up, w_down, idx, tm, tf, tn):
    t = x.shape[0]
    h, top_i, gates = router(x, g, w_router, 640)
    ids = top_i.reshape(-1)
    onehot = (ids[:, None] == jnp.arange(N_EXPERTS)[None, :]).astype(jnp.int32)
    counts = jnp.sum(onehot, axis=0)
    rank = jnp.take_along_axis(jnp.cumsum(onehot, axis=0) - onehot, ids[:, None], axis=1)[:, 0]
    tiles = (counts + tm - 1) // tm
    tile_end = jnp.cumsum(tiles)
    group_start = (tile_end - tiles) * tm
    pos = group_start[ids] + rank
    n_tiles = (TOP_K * t) // tm + N_EXPERTS
    slot_row = jnp.zeros((n_tiles * tm,), jnp.int32).at[pos].set(jnp.arange(TOP_K * t, dtype=jnp.int32) // TOP_K)
    tile_idx = jnp.arange(n_tiles, dtype=jnp.int32)
    tile_live = (tile_idx < tile_end[-1]).astype(jnp.int32)
    tile_expert = jnp.minimum(jnp.sum((tile_idx[:, None] >= tile_end[None, :]).astype(jnp.int32), axis=1),
                              N_EXPERTS - 1)
    last_live = jnp.take(tile_expert, jnp.maximum(tile_end[-1] - 1, 0))
    tile_expert = jnp.where(tile_live > 0, tile_expert, last_live).astype(jnp.int32)
    xs = gather_rows(h, slot_row, 256, BF16)
    ys = grouped_swiglu(xs, tile_expert, tile_live, w_gate, w_up, w_down, idx, tm, tf, tn)
    return combine_rows(x, gates, ys, pos.astype(jnp.int32), 128)


BRANCH_HEADS = {1: 4, 4: 1, 16: 1}


def kernel(x_prompt, x_sample, cache_a_k, cache_a_v, cache_b_k, cache_b_v, g_mix, w_in, sink_a, g_out_a, g_out_b,
           w_out, g_ffn, w_gate_dense, w_up_dense, w_down_dense, w_router, w_gate_moe, w_up_moe, w_down_moe,
           g_final):
    n_batch, seq, d = x_prompt.shape
    n_seq, t_dec, _ = x_sample.shape
    depth = w_in.shape[0]
    tp = n_batch * seq
    ts = n_seq * t_dec
    t_all = tp + ts
    tm = 640
    assert t_all % tm == 0 and t_dec == T_DEC and tp % BLOCK == 0 and ts == BLOCK
    assert cache_a_k.shape[2] == WINDOW_A and cache_b_k.shape[2] == WINDOW_B_MAX
    keep_a, keep_b = min(WINDOW_A, seq), min(WINDOW_B_MAX, seq)
    x = jnp.concatenate([x_prompt.reshape(tp, d), x_sample.reshape(ts, d)], axis=0)
    prompt_kv = [[] for _ in range(4)]
    sample_kv = [[] for _ in range(4)]
    for l in range(depth):
        h = rmsnorm_rows(x, g_mix[l], BF16, tm, 0, t_all // tm)
        proj = matmul_layer(h, w_in, l, tm, 1536)
        proj_s = proj[tp:]
        oa = band_attention_a(proj, sink_a[l], n_batch, seq)
        obs, lses = zip(*[band_attention_b(proj, n_batch, seq, r, BRANCH_HEADS[r]) for _, r in DILATED_BRANCHES])
        o_p = merge_prompt(oa, obs, lses, g_out_a[l], g_out_b[l], 256)
        o_s = sample_attention(proj_s, l, sink_a[l], cache_a_k, cache_a_v, cache_b_k, cache_b_v,
                               g_out_a[l], g_out_b[l])
        o_all = jnp.concatenate([o_p, o_s.astype(BF16)], axis=0)
        x = matmul_layer(o_all, w_out, l, tm, 1024, residual=x)
        if l % 2 == 0:
            h2 = rmsnorm_rows(x, g_ffn[l], BF16, tm, 0, t_all // tm)
            x = dense_ffn(h2, x, w_gate_dense, w_up_dense, w_down_dense, l // 2, tm, 512, 512)
        else:
            x = moe_ffn(x, g_ffn[l], w_router[l // 2], w_gate_moe, w_up_moe, w_down_moe, l // 2, 512, 1024, 512)
        for n, (c0, heads, keep) in enumerate(((COL_KA, KV_HEADS_A, keep_a), (COL_VA, KV_HEADS_A, keep_a),
                                               (COL_KB, N_HEADS_B, keep_b), (COL_VB, N_HEADS_B, keep_b))):
            c1 = c0 + heads * HEAD_DIM
            prompt_kv[n].append(jnp.stack([proj[(b + 1) * seq - keep:(b + 1) * seq, c0:c1]
                                           for b in range(n_batch)]).reshape(n_batch, keep, heads, HEAD_DIM))
            sample_kv[n].append(proj_s[:, c0:c1].reshape(n_seq, t_dec, heads, HEAD_DIM))
    y_prompt = rmsnorm_rows(x, g_final, F32, 512, 0, tp // 512).reshape(n_batch, seq, d)
    y_sample = rmsnorm_rows(x, g_final, F32, ts, tp // ts, 1).reshape(n_seq, t_dec, d)
    new_sample = [cache_shift(c, jnp.stack(rows))
                  for c, rows in zip((cache_a_k, cache_a_v, cache_b_k, cache_b_v), sample_kv)]
    return (y_prompt, y_sample) + tuple(jnp.stack(p) for p in prompt_kv) + tuple(new_sample)
```

```python
import functools

import jax
import jax.numpy as jnp
import numpy as np
from jax import lax
from jax.experimental import pallas as pl
from jax.experimental.pallas import tpu as pltpu

F32 = jnp.float32
BF16 = jnp.bfloat16

D_MODEL = 2048
HEAD_DIM = 128
N_HEADS_A = 8
KV_HEADS_A = 2
GROUP_A = 4
N_HEADS_B = 8
WIDTH_A = N_HEADS_A * HEAD_DIM
KV_WIDTH_A = KV_HEADS_A * HEAD_DIM
WIDTH_B = N_HEADS_B * HEAD_DIM
PROJ_WIDTH = WIDTH_A + 2 * KV_WIDTH_A + 3 * WIDTH_B
COL_KA = WIDTH_A
COL_VA = COL_KA + KV_WIDTH_A
COL_QB = COL_VA + KV_WIDTH_A
COL_KB = COL_QB + WIDTH_B
COL_VB = COL_KB + WIDTH_B
WINDOW_A = 128
DILATED_BRANCHES = ((128, 1), (512, 4), (2048, 16))
WINDOW_B_MAX = 2048
BLOCK = 128
N_EXPERTS = 8
TOP_K = 2
EPS = 1e-5
NEG_INF = -1e30
SCALE = HEAD_DIM ** -0.5
SLOPES = tuple(2.0 ** (-8.0 * i / (N_HEADS_A + N_HEADS_B)) for i in range(1, N_HEADS_A + N_HEADS_B + 1))
SLOPES_A = SLOPES[:N_HEADS_A]
SLOPES_B = SLOPES[N_HEADS_A:]

MIB = 1024 * 1024


def _params(semantics, vmem_mib):
    return pltpu.CompilerParams(dimension_semantics=semantics, vmem_limit_bytes=vmem_mib * MIB)


def _bdot(a, b):
    return jnp.dot(a.astype(BF16), b.astype(BF16), preferred_element_type=F32)


def _bdot_nt(a, b):
    return lax.dot_general(a.astype(BF16), b.astype(BF16), (((1,), (1,)), ((), ())),
                           preferred_element_type=F32)


def _rms(x, g):
    return x * lax.rsqrt(jnp.mean(x * x, axis=-1, keepdims=True) + EPS) * g


def _rmsnorm_kernel(x_ref, g_ref, o_ref):
    o_ref[...] = _rms(x_ref[...], g_ref[...]).astype(o_ref.dtype)


def rmsnorm_rows(x, g, out_dtype, row_block, first_block, n_blocks):
    d = x.shape[1]
    return pl.pallas_call(
        _rmsnorm_kernel,
        grid=(n_blocks,),
        in_specs=[pl.BlockSpec((row_block, d), lambda i: (i + first_block, 0)),
                  pl.BlockSpec((1, d), lambda i: (0, 0))],
        out_specs=pl.BlockSpec((row_block, d), lambda i: (i, 0)),
        out_shape=jax.ShapeDtypeStruct((n_blocks * row_block, d), out_dtype),
        compiler_params=_params(("arbitrary",), 40),
        name="rmsnorm",
    )(x, g.reshape(1, d))


def _matmul_kernel(x_ref, w_ref, o_ref, wb_ref):
    @pl.when(pl.program_id(1) == 0)
    def _():
        wb_ref[...] = w_ref[...].astype(BF16)

    o_ref[...] = jnp.dot(x_ref[...], wb_ref[...], preferred_element_type=F32)


def _matmul_res_kernel(x_ref, w_ref, r_ref, o_ref, wb_ref):
    @pl.when(pl.program_id(1) == 0)
    def _():
        wb_ref[...] = w_ref[...].astype(BF16)

    o_ref[...] = r_ref[...] + jnp.dot(x_ref[...], wb_ref[...], preferred_element_type=F32)


def matmul_layer(x, w_all, layer, tm, tn, residual=None):
    t, k = x.shape
    n = w_all.shape[2]
    grid = (n // tn, t // tm)
    in_specs = [pl.BlockSpec((tm, k), lambda j, i: (i, 0)),
                pl.BlockSpec((None, k, tn), lambda j, i: (layer, 0, j))]
    args = [x, w_all]
    kern = _matmul_kernel
    if residual is not None:
        in_specs.append(pl.BlockSpec((tm, tn), lambda j, i: (i, j)))
        args.append(residual)
        kern = _matmul_res_kernel
    return pl.pallas_call(
        kern,
        grid=grid,
        in_specs=in_specs,
        out_specs=pl.BlockSpec((tm, tn), lambda j, i: (i, j)),
        out_shape=jax.ShapeDtypeStruct((t, n), F32),
        scratch_shapes=[pltpu.VMEM((k, tn), BF16)],
        compiler_params=_params(("arbitrary", "arbitrary"), 56),
        name="matmul",
    )(*args)


def _band_mask(has_prev):
    i = lax.broadcasted_iota(jnp.int32, (BLOCK, 2 * BLOCK), 0)
    j = lax.broadcasted_iota(jnp.int32, (BLOCK, 2 * BLOCK), 1)
    dist = BLOCK + i - j
    valid = (dist >= 0) & (dist <= BLOCK) & ((j >= BLOCK) | has_prev)
    return dist.astype(F32), valid


def _band_head(q, kp, kc, vp, vc, slope, dist_f, valid, sink):
    k = jnp.concatenate([kp, kc], axis=0)
    v = jnp.concatenate([vp, vc], axis=0)
    s = _bdot_nt(q, k) * SCALE - slope * dist_f
    s = jnp.where(valid, s, NEG_INF)
    m = jnp.max(s, axis=-1, keepdims=True)
    if sink is not None:
        m = jnp.maximum(m, sink)
    p = jnp.exp(s - m)
    l = jnp.sum(p, axis=-1, keepdims=True)
    if sink is not None:
        l = l + jnp.exp(sink - m)
    return _bdot(p, v) / l, m + jnp.log(l)


def _band_kernel(*refs, dilation, sub_chunks, heads, kv_group, has_sink, with_lse):
    refs = list(refs)
    sink_ref = refs.pop(0) if has_sink else None
    slope_ref, q_ref, kp_ref, kc_ref, vp_ref, vc_ref, o_ref = refs[:7]
    lse_ref = refs[7] if with_lse else None
    hg = pl.program_id(1)
    dist_f, valid_first = _band_mask(pl.program_id(2) > 0)
    _, valid_rest = _band_mask(True)
    span = BLOCK * dilation

    def rows_of(start, res):
        return pl.ds(start + res, BLOCK, stride=dilation) if dilation > 1 else pl.ds(start, BLOCK)

    for sc in range(sub_chunks):
        for res in range(dilation):
            rows = rows_of(sc * span, res)
            for hh in range(heads):
                cols = slice(hh * HEAD_DIM, (hh + 1) * HEAD_DIM)
                kcols = slice((hh // kv_group) * HEAD_DIM, (hh // kv_group + 1) * HEAD_DIM)
                if sc == 0:
                    kp, vp = kp_ref[rows_of(0, res), kcols], vp_ref[rows_of(0, res), kcols]
                else:
                    prev = rows_of((sc - 1) * span, res)
                    kp, vp = kc_ref[prev, kcols], vc_ref[prev, kcols]
                head = hg * heads + hh
                o, lse = _band_head(q_ref[rows, cols], kp, kc_ref[rows, kcols], vp, vc_ref[rows, kcols],
                                    slope_ref[0, head], dist_f, valid_first if sc == 0 else valid_rest,
                                    sink_ref[0, head] if has_sink else None)
                o_ref[rows, cols] = o
                if with_lse:
                    lse_ref[rows, cols] = jnp.broadcast_to(lse, (BLOCK, HEAD_DIM))


def band_attention(proj, n_batch, seq, *, dilation, sub_chunks, heads, kv_heads, n_heads, q_col, k_col, v_col,
                   slopes, sink=None, with_lse=False, name):
    r = dilation
    span = BLOCK * r
    chunk = span * sub_chunks
    nc = seq // chunk
    w, kvw = heads * HEAD_DIM, kv_heads * HEAD_DIM
    assert seq % chunk == 0 and q_col % w == 0 and k_col % kvw == 0 and v_col % kvw == 0

    def row(b, hg, ic):
        return b * nc + ic

    def prev(b, hg, ic):
        return b * nc * sub_chunks + jnp.maximum(ic * sub_chunks - 1, 0)

    kern = functools.partial(_band_kernel, dilation=r, sub_chunks=sub_chunks, heads=heads,
                             kv_group=heads // kv_heads, has_sink=sink is not None, with_lse=with_lse)
    smem = pl.BlockSpec(memory_space=pltpu.SMEM)
    in_specs = [smem,
                pl.BlockSpec((chunk, w), lambda b, hg, ic: (row(b, hg, ic), q_col // w + hg)),
                pl.BlockSpec((span, kvw), lambda b, hg, ic: (prev(b, hg, ic), k_col // kvw + hg)),
                pl.BlockSpec((chunk, kvw), lambda b, hg, ic: (row(b, hg, ic), k_col // kvw + hg)),
                pl.BlockSpec((span, kvw), lambda b, hg, ic: (prev(b, hg, ic), v_col // kvw + hg)),
                pl.BlockSpec((chunk, kvw), lambda b, hg, ic: (row(b, hg, ic), v_col // kvw + hg))]
    args = [jnp.asarray([slopes], F32), proj, proj, proj, proj, proj]
    if sink is not None:
        in_specs.insert(0, smem)
        args.insert(0, sink.reshape(1, n_heads))
    out_sds = jax.ShapeDtypeStruct((n_batch * seq, n_heads * HEAD_DIM), F32)
    out_spec = pl.BlockSpec((chunk, w), lambda b, hg, ic: (row(b, hg, ic), hg))
    return pl.pallas_call(
        kern,
        grid=(n_batch, n_heads // heads, nc),
        in_specs=in_specs,
        out_specs=[out_spec, out_spec] if with_lse else out_spec,
        out_shape=[out_sds, out_sds] if with_lse else out_sds,
        compiler_params=_params(("arbitrary",) * 3, 40),
        name=name,
    )(*args)


def band_attention_a(proj, sink, n_batch, seq):
    return band_attention(proj, n_batch, seq, dilation=1, sub_chunks=2, heads=N_HEADS_A, kv_heads=KV_HEADS_A,
                          n_heads=N_HEADS_A, q_col=0, k_col=COL_KA, v_col=COL_VA, slopes=SLOPES_A, sink=sink,
                          name="band_attn_a")


BRANCH_TILING = {1: (4, 4), 4: (4, 1), 16: (1, 1)}


def band_attention_b(proj, n_batch, seq, dilation):
    sub_chunks, heads = BRANCH_TILING[dilation]
    return band_attention(proj, n_batch, seq, dilation=dilation, sub_chunks=sub_chunks, heads=heads,
                          kv_heads=heads, n_heads=N_HEADS_B, q_col=COL_QB, k_col=COL_KB, v_col=COL_VB,
                          slopes=[s * dilation for s in SLOPES_B], with_lse=True, name="band_attn_b")


def _merge_kernel(oa_ref, o1_ref, o2_ref, o3_ref, l1_ref, l2_ref, l3_ref, ga_ref, gb_ref, out_ref):
    l1, l2, l3 = l1_ref[...], l2_ref[...], l3_ref[...]
    m = jnp.maximum(jnp.maximum(l1, l2), l3)
    e1, e2, e3 = jnp.exp(l1 - m), jnp.exp(l2 - m), jnp.exp(l3 - m)
    ob = (e1 * o1_ref[...] + e2 * o2_ref[...] + e3 * o3_ref[...]) / (e1 + e2 + e3)
    out_ref[:, :WIDTH_A] = _rms(oa_ref[...], ga_ref[...]).astype(out_ref.dtype)
    out_ref[:, WIDTH_A:] = _rms(ob, gb_ref[...]).astype(out_ref.dtype)


def merge_prompt(oa, obs, lses, g_a, g_b, row_block):
    t = oa.shape[0]
    spec = pl.BlockSpec((row_block, WIDTH_A), lambda i: (i, 0))
    gspec = pl.BlockSpec((1, WIDTH_A), lambda i: (0, 0))
    return pl.pallas_call(
        _merge_kernel,
        grid=(t // row_block,),
        in_specs=[spec] * 7 + [gspec, gspec],
        out_specs=pl.BlockSpec((row_block, WIDTH_A + WIDTH_B), lambda i: (i, 0)),
        out_shape=jax.ShapeDtypeStruct((t, WIDTH_A + WIDTH_B), BF16),
        compiler_params=_params(("arbitrary",), 40),
        name="merge_prompt",
    )(oa, *obs, *lses, g_a.reshape(1, WIDTH_A), g_b.reshape(1, WIDTH_B))


T_DEC = 4
ROWS_16 = WINDOW_B_MAX // 16
ROWS_4 = 512


def _select_by_index(idx, values):
    out = values[-1]
    for n in range(len(values) - 2, -1, -1):
        out = jnp.where(idx == n, values[n], out)
    return out


def _sample_group_a(sink_ref, qa_ref, kn_ref, cak_ref, cav_ref, ga_ref, oa_ref):
    kn = kn_ref[0]
    rows_a = GROUP_A * T_DEC
    for hk in range(KV_HEADS_A):
        q = qa_ref[0, hk * rows_a:(hk + 1) * rows_a, :]
        row = lax.broadcasted_iota(jnp.int32, (rows_a, 1), 0)
        t_row, g_row = row & (T_DEC - 1), row >> 2
        slope = _select_by_index(g_row, [SLOPES_A[hk * GROUP_A + g] for g in range(GROUP_A)])
        sink = _select_by_index(g_row, [sink_ref[0, hk * GROUP_A + g] for g in range(GROUP_A)])
        kc = cak_ref[0, 0, :, hk * HEAD_DIM:(hk + 1) * HEAD_DIM]
        vc = cav_ref[0, 0, :, hk * HEAD_DIM:(hk + 1) * HEAD_DIM]
        c = lax.broadcasted_iota(jnp.int32, (rows_a, WINDOW_A), 1)
        dist = t_row + WINDOW_A - c
        s_c = _bdot_nt(q, kc) * SCALE - slope * dist.astype(F32)
        s_c = jnp.where(dist <= WINDOW_A, s_c, NEG_INF)
        news = []
        for tp in range(T_DEC):
            k_row = kn[tp:tp + 1, COL_KA + hk * HEAD_DIM:COL_KA + (hk + 1) * HEAD_DIM]
            v_row = kn[tp:tp + 1, COL_VA + hk * HEAD_DIM:COL_VA + (hk + 1) * HEAD_DIM]
            s = jnp.sum(q * k_row, axis=-1, keepdims=True) * SCALE - slope * (t_row - tp).astype(F32)
            news.append((jnp.where(t_row >= tp, s, NEG_INF), v_row))
        m = jnp.maximum(jnp.max(s_c, axis=-1, keepdims=True), sink)
        for s_n, _ in news:
            m = jnp.maximum(m, s_n)
        p_c = jnp.exp(s_c - m)
        l = jnp.sum(p_c, axis=-1, keepdims=True) + jnp.exp(sink - m)
        acc = _bdot(p_c, vc)
        for s_n, v_row in news:
            p_n = jnp.exp(s_n - m)
            l = l + p_n
            acc = acc + p_n * v_row
        o = acc / l
        for g in range(GROUP_A):
            col = (hk * GROUP_A + g) * HEAD_DIM
            oa_ref[0, :, col:col + HEAD_DIM] = o[g * T_DEC:(g + 1) * T_DEC, :]
    oa_ref[0] = _rms(oa_ref[0], ga_ref[...])


def _sample_group_b(qb_ref, knb_ref, vnb_ref, ck16_ref, cv16_ref, ck4_ref, cv4_ref, gb_ref, ob_ref):
    head = lax.broadcasted_iota(jnp.int32, (1, N_HEADS_B, 1), 1)
    slope = _select_by_index(head, list(SLOPES_B))
    i3 = lax.broadcasted_iota(jnp.int32, (ROWS_16, 1, 1), 0)
    i3f = i3.astype(F32)
    for t in range(T_DEC):
        q = qb_ref[0, t] * SCALE

        def scores(k3, dist3):
            return jnp.sum(k3 * q[None], axis=-1, keepdims=True) - slope * dist3

        s16 = scores(ck16_ref[0, 0, :, t], WINDOW_B_MAX - 16.0 * i3f)
        s4 = scores(ck4_ref[0, 0, pl.ds(t, ROWS_16, stride=4)], ROWS_4 - 4.0 * i3f)
        s1 = scores(ck4_ref[0, 0, ROWS_4 - BLOCK:ROWS_4], BLOCK + t - i3f)
        s1 = jnp.where(i3 >= t, s1, NEG_INF)
        news = []
        for tp in range(t + 1):
            s = jnp.sum(knb_ref[0, tp] * q, axis=-1, keepdims=True) - slope[0] * float(t - tp)
            news.append((s, 3.0 if tp == t else 1.0, vnb_ref[0, tp]))
        m = jnp.maximum(jnp.maximum(jnp.max(s16, axis=0), jnp.max(s4, axis=0)), jnp.max(s1, axis=0))
        for s_n, _, _ in news:
            m = jnp.maximum(m, s_n)
        p16, p4, p1 = jnp.exp(s16 - m[None]), jnp.exp(s4 - m[None]), jnp.exp(s1 - m[None])
        l = jnp.sum(p16, axis=0) + jnp.sum(p4, axis=0) + jnp.sum(p1, axis=0)
        acc = (jnp.sum(p16 * cv16_ref[0, 0, :, t], axis=0)
               + jnp.sum(p4 * cv4_ref[0, 0, pl.ds(t, ROWS_16, stride=4)], axis=0)
               + jnp.sum(p1 * cv4_ref[0, 0, ROWS_4 - BLOCK:ROWS_4], axis=0))
        for s_n, wgt, v_new in news:
            p_n = wgt * jnp.exp(s_n - m)
            l = l + p_n
            acc = acc + p_n * v_new
        o = acc / l
        ms = jnp.sum(jnp.sum(o * o, axis=1, keepdims=True), axis=0, keepdims=True) / WIDTH_B
        ob_ref[0, t] = o * lax.rsqrt(ms + EPS) * gb_ref[...]


def _sample_attn_kernel(sink_ref, qa_ref, kn_ref, cak_ref, cav_ref, qb_ref, knb_ref, vnb_ref, ck16_ref, cv16_ref,
                        ck4_ref, cv4_ref, ga_ref, gb_ref, oa_ref, ob_ref):
    _sample_group_a(sink_ref, qa_ref, kn_ref, cak_ref, cav_ref, ga_ref, oa_ref)
    _sample_group_b(qb_ref, knb_ref, vnb_ref, ck16_ref, cv16_ref, ck4_ref, cv4_ref, gb_ref, ob_ref)


def sample_attention(proj_s, layer, sink, cache_a_k, cache_a_v, cache_b_k, cache_b_v, g_a, g_b):
    nseq = proj_s.shape[0] // T_DEC
    depth = cache_a_k.shape[0]
    p3 = proj_s.reshape(nseq, T_DEC, PROJ_WIDTH)
    qa = p3[:, :, :WIDTH_A].reshape(nseq, T_DEC, N_HEADS_A, HEAD_DIM).transpose(0, 2, 1, 3)
    qa = qa.reshape(nseq, N_HEADS_A * T_DEC, HEAD_DIM)
    qb = p3[:, :, COL_QB:COL_KB].reshape(nseq, T_DEC, N_HEADS_B, HEAD_DIM)
    knb = p3[:, :, COL_KB:COL_VB].reshape(nseq, T_DEC, N_HEADS_B, HEAD_DIM)
    vnb = p3[:, :, COL_VB:].reshape(nseq, T_DEC, N_HEADS_B, HEAD_DIM)
    ca_k = cache_a_k.reshape(depth, nseq, WINDOW_A, KV_WIDTH_A)
    ca_v = cache_a_v.reshape(depth, nseq, WINDOW_A, KV_WIDTH_A)
    cb16_k = cache_b_k.reshape(depth, nseq, ROWS_16, 16, N_HEADS_B, HEAD_DIM)
    cb16_v = cache_b_v.reshape(depth, nseq, ROWS_16, 16, N_HEADS_B, HEAD_DIM)
    last4 = WINDOW_B_MAX // ROWS_4 - 1
    a_spec = pl.BlockSpec((1, 1, WINDOW_A, KV_WIDTH_A), lambda b: (layer, b, 0, 0))
    b16_spec = pl.BlockSpec((1, 1, ROWS_16, T_DEC, N_HEADS_B, HEAD_DIM), lambda b: (layer, b, 0, 0, 0, 0))
    b4_spec = pl.BlockSpec((1, 1, ROWS_4, N_HEADS_B, HEAD_DIM), lambda b: (layer, b, last4, 0, 0))
    tok_spec = pl.BlockSpec((1, T_DEC, N_HEADS_B, HEAD_DIM), lambda b: (b, 0, 0, 0))
    oa, ob = pl.pallas_call(
        _sample_attn_kernel,
        grid=(nseq,),
        in_specs=[pl.BlockSpec(memory_space=pltpu.SMEM),
                  pl.BlockSpec((1, N_HEADS_A * T_DEC, HEAD_DIM), lambda b: (b, 0, 0)),
                  pl.BlockSpec((1, T_DEC, PROJ_WIDTH), lambda b: (b, 0, 0)),
                  a_spec, a_spec, tok_spec, tok_spec, tok_spec, b16_spec, b16_spec, b4_spec, b4_spec,
                  pl.BlockSpec((1, WIDTH_A), lambda b: (0, 0)),
                  pl.BlockSpec((N_HEADS_B, HEAD_DIM), lambda b: (0, 0))],
        out_specs=[pl.BlockSpec((1, T_DEC, WIDTH_A), lambda b: (b, 0, 0)), tok_spec],
        out_shape=[jax.ShapeDtypeStruct((nseq, T_DEC, WIDTH_A), F32),
                   jax.ShapeDtypeStruct((nseq, T_DEC, N_HEADS_B, HEAD_DIM), F32)],
        compiler_params=_params(("arbitrary",), 48),
        name="sample_attn",
    )(sink.reshape(1, N_HEADS_A), qa, p3, ca_k, ca_v, qb, knb, vnb, cb16_k, cb16_v, cache_b_k, cache_b_v,
      g_a.reshape(1, WIDTH_A), g_b.reshape(N_HEADS_B, HEAD_DIM))
    return jnp.concatenate([oa.reshape(nseq * T_DEC, WIDTH_A), ob.reshape(nseq * T_DEC, WIDTH_B)], axis=1)


CACHE_BLOCK_BYTES = 8 * MIB


def _cache_shift_kernel(cache_ref, new_ref, out_ref):
    rows, t_new = cache_ref.shape[2], new_ref.shape[2]
    out_ref[:, :, :rows - t_new] = cache_ref[:, :, t_new:]
    out_ref[:, :, rows - t_new:] = new_ref[...]


def cache_shift(cache, new_rows):
    depth, n_seq, rows, heads, hd = cache.shape
    t_new = new_rows.shape[2]
    seq_bytes = rows * max(heads, 8) * hd * cache.dtype.itemsize
    nb = max(1, min(n_seq, CACHE_BLOCK_BYTES // seq_bytes))
    assert n_seq % nb == 0
    return pl.pallas_call(
        _cache_shift_kernel,
        grid=(depth, n_seq // nb),
        in_specs=[pl.BlockSpec((1, nb, rows, heads, hd), lambda l, b: (l, b, 0, 0, 0)),
                  pl.BlockSpec((1, nb, t_new, heads, hd), lambda l, b: (l, b, 0, 0, 0))],
        out_specs=pl.BlockSpec((1, nb, rows, heads, hd), lambda l, b: (l, b, 0, 0, 0)),
        out_shape=jax.ShapeDtypeStruct(cache.shape, cache.dtype),
        compiler_params=_params(("arbitrary", "arbitrary"), 48),
        name="cache_shift",
    )(cache, new_rows)


def _group_starts(te_ref, t):
    return (t == 0) | (te_ref[t] != te_ref[jnp.maximum(t - 1, 0)])


def _gate_up_kernel(te_ref, tv_ref, x_ref, wg_ref, wu_ref, a_ref, wgb_ref, wub_ref):
    t = pl.program_id(1)
    live = tv_ref[t] > 0

    @pl.when(live & _group_starts(te_ref, t))
    def _():
        wgb_ref[...] = wg_ref[...].astype(BF16)
        wub_ref[...] = wu_ref[...].astype(BF16)

    @pl.when(live)
    def _():
        x = x_ref[...]
        g = jnp.dot(x, wgb_ref[...], preferred_element_type=F32)
        u = jnp.dot(x, wub_ref[...], preferred_element_type=F32)
        a_ref[...] = (g * jax.nn.sigmoid(g) * u).astype(BF16)

    @pl.when(jnp.logical_not(live))
    def _():
        a_ref[...] = jnp.zeros_like(a_ref)


def _down_kernel(te_ref, tv_ref, a_ref, wd_ref, *rest):
    r_ref = rest[0] if len(rest) == 3 else None
    o_ref, wdb_ref = rest[-2:]
    t = pl.program_id(1)
    live = tv_ref[t] > 0

    @pl.when(live & _group_starts(te_ref, t))
    def _():
        wdb_ref[...] = wd_ref[...].astype(BF16)

    @pl.when(live)
    def _():
        y = jnp.dot(a_ref[...], wdb_ref[...], preferred_element_type=F32)
        o_ref[...] = y if r_ref is None else r_ref[...] + y

    @pl.when(jnp.logical_not(live))
    def _():
        o_ref[...] = jnp.zeros_like(o_ref) if r_ref is None else r_ref[...]


def grouped_swiglu(x, tile_group, tile_live, w_gate, w_up, w_down, idx, tm, tf, tn, residual=None):
    p, d = x.shape
    d_ff = w_gate.shape[3]
    n_tiles = p // tm
    act = pl.pallas_call(
        _gate_up_kernel,
        grid_spec=pltpu.PrefetchScalarGridSpec(
            num_scalar_prefetch=2,
            grid=(d_ff // tf, n_tiles),
            in_specs=[pl.BlockSpec((tm, d), lambda f, t, te, tv: (t, 0)),
                      pl.BlockSpec((None, None, d, tf), lambda f, t, te, tv: (idx, te[t], 0, f)),
                      pl.BlockSpec((None, None, d, tf), lambda f, t, te, tv: (idx, te[t], 0, f))],
            out_specs=pl.BlockSpec((tm, tf), lambda f, t, te, tv: (t, f)),
            scratch_shapes=[pltpu.VMEM((d, tf), BF16), pltpu.VMEM((d, tf), BF16)]),
        out_shape=jax.ShapeDtypeStruct((p, d_ff), BF16),
        compiler_params=_params(("arbitrary", "arbitrary"), 60),
        name="swiglu_gate_up",
    )(tile_group, tile_live, x, w_gate, w_up)
    in_specs = [pl.BlockSpec((tm, d_ff), lambda n, t, te, tv: (t, 0)),
                pl.BlockSpec((None, None, d_ff, tn), lambda n, t, te, tv: (idx, te[t], 0, n))]
    args = [act, w_down]
    if residual is not None:
        in_specs.append(pl.BlockSpec((tm, tn), lambda n, t, te, tv: (t, n)))
        args.append(residual)
    return pl.pallas_call(
        _down_kernel,
        grid_spec=pltpu.PrefetchScalarGridSpec(
            num_scalar_prefetch=2,
            grid=(d // tn, n_tiles),
            in_specs=in_specs,
            out_specs=pl.BlockSpec((tm, tn), lambda n, t, te, tv: (t, n)),
            scratch_shapes=[pltpu.VMEM((d_ff, tn), BF16)]),
        out_shape=jax.ShapeDtypeStruct((p, d), F32),
        compiler_params=_params(("arbitrary", "arbitrary"), 60),
        name="swiglu_down",
    )(tile_group, tile_live, *args)


def dense_ffn(h, x, w_gate, w_up, w_down, idx, tm, tf, tn):
    n_tiles = h.shape[0] // tm
    one_group = jnp.zeros((n_tiles,), jnp.int32)
    all_live = jnp.ones((n_tiles,), jnp.int32)
    return grouped_swiglu(h, one_group, all_live, w_gate[:, None], w_up[:, None], w_down[:, None], idx,
                          tm, tf, tn, residual=x)


ROUTE_LANES = 128
LANE_CHUNKS = D_MODEL // 128


def _router_kernel(x_ref, g_ref, wr_ref, h_ref, route_ref):
    h = _rms(x_ref[...], g_ref[...])
    for c in range(LANE_CHUNKS):
        h_ref[pl.ds(c, h.shape[0], stride=LANE_CHUNKS), :] = h[:, c * HEAD_DIM:(c + 1) * HEAD_DIM]
    logits = jnp.dot(h, wr_ref[...], preferred_element_type=F32, precision=lax.Precision.HIGHEST)
    lane = lax.broadcasted_iota(jnp.int32, logits.shape, 1).astype(F32)
    lg = jnp.where(lane < N_EXPERTS, logits, -jnp.inf)
    m1 = jnp.max(lg, axis=-1, keepdims=True)
    i1 = jnp.min(jnp.where(lg == m1, lane, float(ROUTE_LANES)), axis=-1, keepdims=True)
    lg2 = jnp.where(lane == i1, -jnp.inf, lg)
    m2 = jnp.max(lg2, axis=-1, keepdims=True)
    i2 = jnp.min(jnp.where(lg2 == m2, lane, float(ROUTE_LANES)), axis=-1, keepdims=True)
    e = jnp.exp(m2 - m1)
    g1 = 1.0 / (1.0 + e)
    g2 = e / (1.0 + e)
    route_ref[...] = jnp.where(lane == 0, i1, jnp.where(lane == 1, i2,
                                                        jnp.where(lane == 2, g1, jnp.where(lane == 3, g2, 0.0))))


def router(x, g, w_router, tm):
    t, d = x.shape
    wr = jnp.pad(w_router, ((0, 0), (0, ROUTE_LANES - N_EXPERTS)))
    h, route = pl.pallas_call(
        _router_kernel,
        grid=(t // tm,),
        in_specs=[pl.BlockSpec((tm, d), lambda i: (i, 0)),
                  pl.BlockSpec((1, d), lambda i: (0, 0)),
                  pl.BlockSpec((d, ROUTE_LANES), lambda i: (0, 0))],
        out_specs=[pl.BlockSpec((tm * LANE_CHUNKS, 128), lambda i: (i, 0)),
                   pl.BlockSpec((tm, ROUTE_LANES), lambda i: (i, 0))],
        out_shape=[jax.ShapeDtypeStruct((t * LANE_CHUNKS, 128), F32), jax.ShapeDtypeStruct((t, ROUTE_LANES), F32)],
        compiler_params=_params(("arbitrary",), 40),
        name="router",
    )(x, g.reshape(1, d), wr)
    top_i = route[:, :TOP_K].astype(jnp.int32)
    gates = route[:, TOP_K:2 * TOP_K]
    return h, top_i, gates


def _row_copy(src_ref, dst_ref, src_row, dst_row, sem):
    return pltpu.make_async_copy(src_ref.at[pl.ds(src_row, 1)], dst_ref.at[pl.ds(dst_row, 1)], sem)


def _gather_kernel(idx_ref, src_ref, o_ref, buf_ref, sem):
    rows = o_ref.shape[0]
    base = pl.program_id(0) * rows

    def token_copy(r):
        src_row = pl.multiple_of(idx_ref[base + r] * LANE_CHUNKS, LANE_CHUNKS)
        dst_row = pl.multiple_of(r * LANE_CHUNKS, LANE_CHUNKS)
        return pltpu.make_async_copy(src_ref.at[pl.ds(src_row, LANE_CHUNKS)],
                                     buf_ref.at[pl.ds(dst_row, LANE_CHUNKS)], sem)

    def start(r, carry):
        token_copy(r).start()
        return carry

    def wait(r, carry):
        token_copy(r).wait()
        return carry

    lax.fori_loop(0, rows, start, 0)
    lax.fori_loop(0, rows, wait, 0)
    for c in range(LANE_CHUNKS):
        o_ref[:, c * 128:(c + 1) * 128] = buf_ref[pl.ds(c, rows, stride=LANE_CHUNKS), :].astype(o_ref.dtype)


def gather_rows(src, idx, rows_per_step, out_dtype):
    n = idx.shape[0]
    d = LANE_CHUNKS * src.shape[1]
    return pl.pallas_call(
        _gather_kernel,
        grid_spec=pltpu.PrefetchScalarGridSpec(
            num_scalar_prefetch=1,
            grid=(n // rows_per_step,),
            in_specs=[pl.BlockSpec(memory_space=pl.ANY)],
            out_specs=pl.BlockSpec((rows_per_step, d), lambda i, idx_ref: (i, 0)),
            scratch_shapes=[pltpu.VMEM((rows_per_step * LANE_CHUNKS, src.shape[1]), src.dtype),
                            pltpu.SemaphoreType.DMA(())]),
        out_shape=jax.ShapeDtypeStruct((n, d), out_dtype),
        compiler_params=_params(("arbitrary",), 32),
        name="gather_rows",
    )(idx, src)


def _combine_kernel(pos_ref, x_ref, gate_ref, ys_ref, o_ref, buf_ref, sem):
    rows = x_ref.shape[0]
    base = pl.program_id(0) * rows

    def copies(r):
        return [_row_copy(ys_ref, buf_ref.at[k], pos_ref[TOP_K * (base + r) + k], r, sem) for k in range(TOP_K)]

    def start(r, carry):
        for cp in copies(r):
            cp.start()
        return carry

    def wait(r, carry):
        for cp in copies(r):
            cp.wait()
        return carry

    lax.fori_loop(0, rows, start, 0)
    lax.fori_loop(0, rows, wait, 0)
    gates = gate_ref[...]
    y = x_ref[...]
    for k in range(TOP_K):
        y = y + gates[:, k:k + 1] * buf_ref[k]
    o_ref[...] = y


def combine_rows(x, gates, ys, pos, rows_per_step):
    t, d = x.shape
    return pl.pallas_call(
        _combine_kernel,
        grid_spec=pltpu.PrefetchScalarGridSpec(
            num_scalar_prefetch=1,
            grid=(t // rows_per_step,),
            in_specs=[pl.BlockSpec((rows_per_step, d), lambda i, pos_ref: (i, 0)),
                      pl.BlockSpec((rows_per_step, TOP_K), lambda i, pos_ref: (i, 0)),
                      pl.BlockSpec(memory_space=pl.ANY)],
            out_specs=pl.BlockSpec((rows_per_step, d), lambda i, pos_ref: (i, 0)),
            scratch_shapes=[pltpu.VMEM((TOP_K, rows_per_step, d), F32), pltpu.SemaphoreType.DMA(())]),
        out_shape=jax.ShapeDtypeStruct((t, d), F32),
        compiler_params=_params(("arbitrary",), 32),
        name="combine_rows",
    )(pos, x, gates, ys)


def moe_ffn(x, g, w_router, w_gate, w_up, w_down, idx, tm, tf, tn):
    t = x.shape[0]
    h, top_i, gates = router(x, g, w_router, 640)
    ids = top_i.reshape(-1)
    onehot = (ids[:, None] == jnp.arange(N_EXPERTS)[None, :]).astype(jnp.int32)
    counts = jnp.sum(onehot, axis=0)
    rank = jnp.take_along_axis(jnp.cumsum(onehot, axis=0) - onehot, ids[:, None], axis=1)[:, 0]
    tiles = (counts + tm - 1) // tm
    tile_end = jnp.cumsum(tiles)
    group_start = (tile_end - tiles) * tm
    pos = group_start[ids] + rank
    n_tiles = (TOP_K * t) // tm + N_EXPERTS
    slot_row = jnp.zeros((n_tiles * tm,), jnp.int32).at[pos].set(jnp.arange(TOP_K * t, dtype=jnp.int32) // TOP_K)
    tile_idx = jnp.arange(n_tiles, dtype=jnp.int32)
    tile_live = (tile_idx < tile_end[-1]).astype(jnp.int32)
    tile_expert = jnp.minimum(jnp.sum((tile_idx[:, None] >= tile_end[None, :]).astype(jnp.int32), axis=1),
                              N_EXPERTS - 1)
    last_live = jnp.take(tile_expert, jnp.maximum(tile_end[-1] - 1, 0))
    tile_expert = jnp.where(tile_live > 0, tile_expert, last_live).astype(jnp.int32)
    xs = gather_rows(h, slot_row, 256, BF16)
    ys = grouped_swiglu(xs, tile_expert, tile_live, w_gate, w_up, w_down, idx, tm, tf, tn)
    return combine_rows(x, gates, ys, pos.astype(jnp.int32), 128)


def kernel(x_prompt, x_sample, cache_a_k, cache_a_v, cache_b_k, cache_b_v, g_mix, w_in, sink_a, g_out_a, g_out_b,
           w_out, g_ffn, w_gate_dense, w_up_dense, w_down_dense, w_router, w_gate_moe, w_up_moe, w_down_moe,
           g_final):
    n_batch, seq, d = x_prompt.shape
    n_seq, t_dec, _ = x_sample.shape
    depth = w_in.shape[0]
    tp = n_batch * seq
    ts = n_seq * t_dec
    t_all = tp + ts
    tm = 640
    assert t_all % tm == 0 and t_dec == T_DEC and tp % BLOCK == 0 and ts == BLOCK
    assert cache_a_k.shape[2] == WINDOW_A and cache_b_k.shape[2] == WINDOW_B_MAX
    keep_a, keep_b = min(WINDOW_A, seq), min(WINDOW_B_MAX, seq)
    x = jnp.concatenate([x_prompt.reshape(tp, d), x_sample.reshape(ts, d)], axis=0)
    prompt_kv = [[] for _ in range(4)]
    sample_kv = [[] for _ in range(4)]
    for l in range(depth):
        h = rmsnorm_rows(x, g_mix[l], BF16, tm, 0, t_all // tm)
        proj = matmul_layer(h, w_in, l, tm, 1536)
        proj_s = proj[tp:]
        oa = band_attention_a(proj, sink_a[l], n_batch, seq)
        obs, lses = zip(*[band_attention_b(proj, n_batch, seq, r) for _, r in DILATED_BRANCHES])
        o_p = merge_prompt(oa, obs, lses, g_out_a[l], g_out_b[l], 256)
        o_s = sample_attention(proj_s, l, sink_a[l], cache_a_k, cache_a_v, cache_b_k, cache_b_v,
                               g_out_a[l], g_out_b[l])
        o_all = jnp.concatenate([o_p, o_s.astype(BF16)], axis=0)
        x = matmul_layer(o_all, w_out, l, tm, 1024, residual=x)
        if l % 2 == 0:
            h2 = rmsnorm_rows(x, g_ffn[l], BF16, tm, 0, t_all // tm)
            x = dense_ffn(h2, x, w_gate_dense, w_up_dense, w_down_dense, l // 2, tm, 512, 512)
        else:
            x = moe_ffn(x, g_ffn[l], w_router[l // 2], w_gate_moe, w_up_moe, w_down_moe, l // 2, 512, 1024, 512)
        for n, (c0, heads, keep) in enumerate(((COL_KA, KV_HEADS_A, keep_a), (COL_VA, KV_HEADS_A, keep_a),
                                               (COL_KB, N_HEADS_B, keep_b), (COL_VB, N_HEADS_B, keep_b))):
            c1 = c0 + heads * HEAD_DIM
            prompt_kv[n].append(jnp.stack([proj[(b + 1) * seq - keep:(b + 1) * seq, c0:c1]
                                           for b in range(n_batch)]).reshape(n_batch, keep, heads, HEAD_DIM))
            sample_kv[n].append(proj_s[:, c0:c1].reshape(n_seq, t_dec, heads, HEAD_DIM))
    y_prompt = rmsnorm_rows(x, g_final, F32, 512, 0, tp // 512).reshape(n_batch, seq, d)
    y_sample = rmsnorm_rows(x, g_final, F32, ts, tp // ts, 1).reshape(n_seq, t_dec, d)
    new_sample = [cache_shift(c, jnp.stack(rows))
                  for c, rows in zip((cache_a_k, cache_a_v, cache_b_k, cache_b_v), sample_kv)]
    return (y_prompt, y_sample) + tuple(jnp.stack(p) for p in prompt_kv) + tuple(new_sample)
```

```python
import functools

import jax
import jax.numpy as jnp
import numpy as np
from jax import lax
from jax.experimental import pallas as pl
from jax.experimental.pallas import tpu as pltpu

F32 = jnp.float32
BF16 = jnp.bfloat16

D_MODEL = 2048
HEAD_DIM = 128
N_HEADS_A = 8
KV_HEADS_A = 2
GROUP_A = 4
N_HEADS_B = 8
WIDTH_A = N_HEADS_A * HEAD_DIM
KV_WIDTH_A = KV_HEADS_A * HEAD_DIM
WIDTH_B = N_HEADS_B * HEAD_DIM
PROJ_WIDTH = WIDTH_A + 2 * KV_WIDTH_A + 3 * WIDTH_B
COL_KA = WIDTH_A
COL_VA = COL_KA + KV_WIDTH_A
COL_QB = COL_VA + KV_WIDTH_A
COL_KB = COL_QB + WIDTH_B
COL_VB = COL_KB + WIDTH_B
WINDOW_A = 128
DILATED_BRANCHES = ((128, 1), (512, 4), (2048, 16))
WINDOW_B_MAX = 2048
BLOCK = 128
N_EXPERTS = 8
TOP_K = 2
EPS = 1e-5
NEG_INF = -1e30
SCALE = HEAD_DIM ** -0.5
SLOPES = tuple(2.0 ** (-8.0 * i / (N_HEADS_A + N_HEADS_B)) for i in range(1, N_HEADS_A + N_HEADS_B + 1))
SLOPES_A = SLOPES[:N_HEADS_A]
SLOPES_B = SLOPES[N_HEADS_A:]

MIB = 1024 * 1024


def _params(semantics, vmem_mib):
    return pltpu.CompilerParams(dimension_semantics=semantics, vmem_limit_bytes=vmem_mib * MIB)


def _bdot(a, b):
    return jnp.dot(a.astype(BF16), b.astype(BF16), preferred_element_type=F32)


def _bdot_nt(a, b):
    return lax.dot_general(a.astype(BF16), b.astype(BF16), (((1,), (1,)), ((), ())),
                           preferred_element_type=F32)


def _rms(x, g):
    return x * lax.rsqrt(jnp.mean(x * x, axis=-1, keepdims=True) + EPS) * g


def _rmsnorm_kernel(x_ref, g_ref, o_ref):
    o_ref[...] = _rms(x_ref[...], g_ref[...]).astype(o_ref.dtype)


def rmsnorm_rows(x, g, out_dtype, row_block, first_block, n_blocks):
    d = x.shape[1]
    return pl.pallas_call(
        _rmsnorm_kernel,
        grid=(n_blocks,),
        in_specs=[pl.BlockSpec((row_block, d), lambda i: (i + first_block, 0)),
                  pl.BlockSpec((1, d), lambda i: (0, 0))],
        out_specs=pl.BlockSpec((row_block, d), lambda i: (i, 0)),
        out_shape=jax.ShapeDtypeStruct((n_blocks * row_block, d), out_dtype),
        compiler_params=_params(("arbitrary",), 40),
        name="rmsnorm",
    )(x, g.reshape(1, d))


def _matmul_kernel(x_ref, w_ref, o_ref, wb_ref):
    @pl.when(pl.program_id(1) == 0)
    def _():
        wb_ref[...] = w_ref[...].astype(BF16)

    o_ref[...] = jnp.dot(x_ref[...], wb_ref[...], preferred_element_type=F32)


def _matmul_res_kernel(x_ref, w_ref, r_ref, o_ref, wb_ref):
    @pl.when(pl.program_id(1) == 0)
    def _():
        wb_ref[...] = w_ref[...].astype(BF16)

    o_ref[...] = r_ref[...] + jnp.dot(x_ref[...], wb_ref[...], preferred_element_type=F32)


def matmul_layer(x, w_all, layer, tm, tn, residual=None):
    t, k = x.shape
    n = w_all.shape[2]
    grid = (n // tn, t // tm)
    in_specs = [pl.BlockSpec((tm, k), lambda j, i: (i, 0)),
                pl.BlockSpec((None, k, tn), lambda j, i: (layer, 0, j))]
    args = [x, w_all]
    kern = _matmul_kernel
    if residual is not None:
        in_specs.append(pl.BlockSpec((tm, tn), lambda j, i: (i, j)))
        args.append(residual)
        kern = _matmul_res_kernel
    return pl.pallas_call(
        kern,
        grid=grid,
        in_specs=in_specs,
        out_specs=pl.BlockSpec((tm, tn), lambda j, i: (i, j)),
        out_shape=jax.ShapeDtypeStruct((t, n), F32),
        scratch_shapes=[pltpu.VMEM((k, tn), BF16)],
        compiler_params=_params(("arbitrary", "arbitrary"), 56),
        name="matmul",
    )(*args)


def _band_mask(has_prev):
    i = lax.broadcasted_iota(jnp.int32, (BLOCK, 2 * BLOCK), 0)
    j = lax.broadcasted_iota(jnp.int32, (BLOCK, 2 * BLOCK), 1)
    dist = BLOCK + i - j
    valid = (dist >= 0) & (dist <= BLOCK) & ((j >= BLOCK) | has_prev)
    return dist.astype(F32), valid


def _band_head(q, kp, kc, vp, vc, slope, dist_f, valid, sink):
    k = jnp.concatenate([kp, kc], axis=0)
    v = jnp.concatenate([vp, vc], axis=0)
    s = _bdot_nt(q, k) * SCALE - slope * dist_f
    s = jnp.where(valid, s, NEG_INF)
    m = jnp.max(s, axis=-1, keepdims=True)
    if sink is not None:
        m = jnp.maximum(m, sink)
    p = jnp.exp(s - m)
    l = jnp.sum(p, axis=-1, keepdims=True)
    if sink is not None:
        l = l + jnp.exp(sink - m)
    return _bdot(p, v) / l, m + jnp.log(l)


def _band_kernel(*refs, dilation, sub_chunks, heads, kv_group, has_sink, with_lse):
    refs = list(refs)
    sink_ref = refs.pop(0) if has_sink else None
    slope_ref, q_ref, kp_ref, kc_ref, vp_ref, vc_ref, o_ref = refs[:7]
    lse_ref = refs[7] if with_lse else None
    hg = pl.program_id(1)
    dist_f, valid_first = _band_mask(pl.program_id(2) > 0)
    _, valid_rest = _band_mask(True)
    span = BLOCK * dilation
    if dilation == STAGED_DILATION:
        _band_staged(refs[-1], slope_ref, q_ref, kp_ref, kc_ref, vp_ref, vc_ref, o_ref, lse_ref, hg,
                     dist_f, valid_first)
        return

    def rows_of(start, res):
        return pl.ds(start + res, BLOCK, stride=dilation) if dilation > 1 else pl.ds(start, BLOCK)

    for sc in range(sub_chunks):
        for res in range(dilation):
            rows = rows_of(sc * span, res)
            for hh in range(heads):
                cols = slice(hh * HEAD_DIM, (hh + 1) * HEAD_DIM)
                kcols = slice((hh // kv_group) * HEAD_DIM, (hh // kv_group + 1) * HEAD_DIM)
                if sc == 0:
                    kp, vp = kp_ref[rows_of(0, res), kcols], vp_ref[rows_of(0, res), kcols]
                else:
                    prev = rows_of((sc - 1) * span, res)
                    kp, vp = kc_ref[prev, kcols], vc_ref[prev, kcols]
                head = hg * heads + hh
                o, lse = _band_head(q_ref[rows, cols], kp, kc_ref[rows, kcols], vp, vc_ref[rows, kcols],
                                    slope_ref[0, head], dist_f, valid_first if sc == 0 else valid_rest,
                                    sink_ref[0, head] if has_sink else None)
                o_ref[rows, cols] = o
                if with_lse:
                    lse_ref[rows, cols] = jnp.broadcast_to(lse, (BLOCK, HEAD_DIM))


STAGED_DILATION = 16
STAGE_STRIDE = 4


def _band_staged(stage_ref, slope_ref, q_ref, kp_ref, kc_ref, vp_ref, vc_ref, o_ref, lse_ref, hg, dist_f, valid):
    n_stage = BLOCK * STAGED_DILATION // STAGE_STRIDE
    for c in range(STAGE_STRIDE):
        outer = pl.ds(c, n_stage, stride=STAGE_STRIDE)
        for n, ref in enumerate((q_ref, kp_ref, kc_ref, vp_ref, vc_ref)):
            stage_ref[n] = ref[outer, :]
        for a in range(STAGED_DILATION // STAGE_STRIDE):
            inner = pl.ds(a, BLOCK, stride=STAGE_STRIDE)
            o, lse = _band_head(stage_ref[0, inner, :], stage_ref[1, inner, :], stage_ref[2, inner, :],
                                stage_ref[3, inner, :], stage_ref[4, inner, :], slope_ref[0, hg], dist_f, valid, None)
            stage_ref[5, inner, :] = o
            stage_ref[6, inner, :] = jnp.broadcast_to(lse, (BLOCK, HEAD_DIM))
        o_ref[outer, :] = stage_ref[5]
        lse_ref[outer, :] = stage_ref[6]


def band_attention(proj, n_batch, seq, *, dilation, sub_chunks, heads, kv_heads, n_heads, q_col, k_col, v_col,
                   slopes, sink=None, with_lse=False, name):
    r = dilation
    span = BLOCK * r
    chunk = span * sub_chunks
    nc = seq // chunk
    w, kvw = heads * HEAD_DIM, kv_heads * HEAD_DIM
    assert seq % chunk == 0 and q_col % w == 0 and k_col % kvw == 0 and v_col % kvw == 0

    def row(b, hg, ic):
        return b * nc + ic

    def prev(b, hg, ic):
        return b * nc * sub_chunks + jnp.maximum(ic * sub_chunks - 1, 0)

    kern = functools.partial(_band_kernel, dilation=r, sub_chunks=sub_chunks, heads=heads,
                             kv_group=heads // kv_heads, has_sink=sink is not None, with_lse=with_lse)
    smem = pl.BlockSpec(memory_space=pltpu.SMEM)
    in_specs = [smem,
                pl.BlockSpec((chunk, w), lambda b, hg, ic: (row(b, hg, ic), q_col // w + hg)),
                pl.BlockSpec((span, kvw), lambda b, hg, ic: (prev(b, hg, ic), k_col // kvw + hg)),
                pl.BlockSpec((chunk, kvw), lambda b, hg, ic: (row(b, hg, ic), k_col // kvw + hg)),
                pl.BlockSpec((span, kvw), lambda b, hg, ic: (prev(b, hg, ic), v_col // kvw + hg)),
                pl.BlockSpec((chunk, kvw), lambda b, hg, ic: (row(b, hg, ic), v_col // kvw + hg))]
    args = [jnp.asarray([slopes], F32), proj, proj, proj, proj, proj]
    if sink is not None:
        in_specs.insert(0, smem)
        args.insert(0, sink.reshape(1, n_heads))
    out_sds = jax.ShapeDtypeStruct((n_batch * seq, n_heads * HEAD_DIM), F32)
    out_spec = pl.BlockSpec((chunk, w), lambda b, hg, ic: (row(b, hg, ic), hg))
    return pl.pallas_call(
        kern,
        grid=(n_batch, n_heads // heads, nc),
        in_specs=in_specs,
        out_specs=[out_spec, out_spec] if with_lse else out_spec,
        out_shape=[out_sds, out_sds] if with_lse else out_sds,
        scratch_shapes=([pltpu.VMEM((7, span // STAGE_STRIDE, HEAD_DIM), F32)]
                        if r == STAGED_DILATION else []),
        compiler_params=_params(("arbitrary",) * 3, 40),
        name=name,
    )(*args)


def band_attention_a(proj, sink, n_batch, seq):
    return band_attention(proj, n_batch, seq, dilation=1, sub_chunks=2, heads=N_HEADS_A, kv_heads=KV_HEADS_A,
                          n_heads=N_HEADS_A, q_col=0, k_col=COL_KA, v_col=COL_VA, slopes=SLOPES_A, sink=sink,
                          name="band_attn_a")


BRANCH_TILING = {1: (4, 4), 4: (4, 1), 16: (1, 1)}


def band_attention_b(proj, n_batch, seq, dilation):
    sub_chunks, heads = BRANCH_TILING[dilation]
    return band_attention(proj, n_batch, seq, dilation=dilation, sub_chunks=sub_chunks, heads=heads,
                          kv_heads=heads, n_heads=N_HEADS_B, q_col=COL_QB, k_col=COL_KB, v_col=COL_VB,
                          slopes=[s * dilation for s in SLOPES_B], with_lse=True, name="band_attn_b")


def _merge_kernel(oa_ref, o1_ref, o2_ref, o3_ref, l1_ref, l2_ref, l3_ref, ga_ref, gb_ref, out_ref):
    l1, l2, l3 = l1_ref[...], l2_ref[...], l3_ref[...]
    m = jnp.maximum(jnp.maximum(l1, l2), l3)
    e1, e2, e3 = jnp.exp(l1 - m), jnp.exp(l2 - m), jnp.exp(l3 - m)
    ob = (e1 * o1_ref[...] + e2 * o2_ref[...] + e3 * o3_ref[...]) / (e1 + e2 + e3)
    out_ref[:, :WIDTH_A] = _rms(oa_ref[...], ga_ref[...]).astype(out_ref.dtype)
    out_ref[:, WIDTH_A:] = _rms(ob, gb_ref[...]).astype(out_ref.dtype)


def merge_prompt(oa, obs, lses, g_a, g_b, row_block):
    t = oa.shape[0]
    spec = pl.BlockSpec((row_block, WIDTH_A), lambda i: (i, 0))
    gspec = pl.BlockSpec((1, WIDTH_A), lambda i: (0, 0))
    return pl.pallas_call(
        _merge_kernel,
        grid=(t // row_block,),
        in_specs=[spec] * 7 + [gspec, gspec],
        out_specs=pl.BlockSpec((row_block, WIDTH_A + WIDTH_B), lambda i: (i, 0)),
        out_shape=jax.ShapeDtypeStruct((t, WIDTH_A + WIDTH_B), BF16),
        compiler_params=_params(("arbitrary",), 40),
        name="merge_prompt",
    )(oa, *obs, *lses, g_a.reshape(1, WIDTH_A), g_b.reshape(1, WIDTH_B))


T_DEC = 4
ROWS_16 = WINDOW_B_MAX // 16
ROWS_4 = 512


def _select_by_index(idx, values):
    out = values[-1]
    for n in range(len(values) - 2, -1, -1):
        out = jnp.where(idx == n, values[n], out)
    return out


def _sample_group_a(sink_ref, qa_ref, kn_ref, cak_ref, cav_ref, ga_ref, oa_ref):
    kn = kn_ref[0]
    rows_a = GROUP_A * T_DEC
    for hk in range(KV_HEADS_A):
        q = qa_ref[0, hk * rows_a:(hk + 1) * rows_a, :]
        row = lax.broadcasted_iota(jnp.int32, (rows_a, 1), 0)
        t_row, g_row = row & (T_DEC - 1), row >> 2
        slope = _select_by_index(g_row, [SLOPES_A[hk * GROUP_A + g] for g in range(GROUP_A)])
        sink = _select_by_index(g_row, [sink_ref[0, hk * GROUP_A + g] for g in range(GROUP_A)])
        kc = cak_ref[0, 0, :, hk * HEAD_DIM:(hk + 1) * HEAD_DIM]
        vc = cav_ref[0, 0, :, hk * HEAD_DIM:(hk + 1) * HEAD_DIM]
        c = lax.broadcasted_iota(jnp.int32, (rows_a, WINDOW_A), 1)
        dist = t_row + WINDOW_A - c
        s_c = _bdot_nt(q, kc) * SCALE - slope * dist.astype(F32)
        s_c = jnp.where(dist <= WINDOW_A, s_c, NEG_INF)
        news = []
        for tp in range(T_DEC):
            k_row = kn[tp:tp + 1, COL_KA + hk * HEAD_DIM:COL_KA + (hk + 1) * HEAD_DIM]
            v_row = kn[tp:tp + 1, COL_VA + hk * HEAD_DIM:COL_VA + (hk + 1) * HEAD_DIM]
            s = jnp.sum(q * k_row, axis=-1, keepdims=True) * SCALE - slope * (t_row - tp).astype(F32)
            news.append((jnp.where(t_row >= tp, s, NEG_INF), v_row))
        m = jnp.maximum(jnp.max(s_c, axis=-1, keepdims=True), sink)
        for s_n, _ in news:
            m = jnp.maximum(m, s_n)
        p_c = jnp.exp(s_c - m)
        l = jnp.sum(p_c, axis=-1, keepdims=True) + jnp.exp(sink - m)
        acc = _bdot(p_c, vc)
        for s_n, v_row in news:
            p_n = jnp.exp(s_n - m)
            l = l + p_n
            acc = acc + p_n * v_row
        o = acc / l
        for g in range(GROUP_A):
            col = (hk * GROUP_A + g) * HEAD_DIM
            oa_ref[0, :, col:col + HEAD_DIM] = o[g * T_DEC:(g + 1) * T_DEC, :]
    oa_ref[0] = _rms(oa_ref[0], ga_ref[...])


def _sample_group_b(qb_ref, knb_ref, vnb_ref, ck16_ref, cv16_ref, ck4_ref, cv4_ref, gb_ref, ob_ref):
    head = lax.broadcasted_iota(jnp.int32, (1, N_HEADS_B, 1), 1)
    slope = _select_by_index(head, list(SLOPES_B))
    i3 = lax.broadcasted_iota(jnp.int32, (ROWS_16, 1, 1), 0)
    i3f = i3.astype(F32)
    for t in range(T_DEC):
        q = qb_ref[0, t] * SCALE

        def scores(k3, dist3):
            return jnp.sum(k3 * q[None], axis=-1, keepdims=True) - slope * dist3

        s16 = scores(ck16_ref[0, 0, :, t], WINDOW_B_MAX - 16.0 * i3f)
        s4 = scores(ck4_ref[0, 0, pl.ds(t, ROWS_16, stride=4)], ROWS_4 - 4.0 * i3f)
        s1 = scores(ck4_ref[0, 0, ROWS_4 - BLOCK:ROWS_4], BLOCK + t - i3f)
        s1 = jnp.where(i3 >= t, s1, NEG_INF)
        news = []
        for tp in range(t + 1):
            s = jnp.sum(knb_ref[0, tp] * q, axis=-1, keepdims=True) - slope[0] * float(t - tp)
            news.append((s, 3.0 if tp == t else 1.0, vnb_ref[0, tp]))
        m = jnp.maximum(jnp.maximum(jnp.max(s16, axis=0), jnp.max(s4, axis=0)), jnp.max(s1, axis=0))
        for s_n, _, _ in news:
            m = jnp.maximum(m, s_n)
        p16, p4, p1 = jnp.exp(s16 - m[None]), jnp.exp(s4 - m[None]), jnp.exp(s1 - m[None])
        l = jnp.sum(p16, axis=0) + jnp.sum(p4, axis=0) + jnp.sum(p1, axis=0)
        acc = (jnp.sum(p16 * cv16_ref[0, 0, :, t], axis=0)
               + jnp.sum(p4 * cv4_ref[0, 0, pl.ds(t, ROWS_16, stride=4)], axis=0)
               + jnp.sum(p1 * cv4_ref[0, 0, ROWS_4 - BLOCK:ROWS_4], axis=0))
        for s_n, wgt, v_new in news:
            p_n = wgt * jnp.exp(s_n - m)
            l = l + p_n
            acc = acc + p_n * v_new
        o = acc / l
        ms = jnp.sum(jnp.sum(o * o, axis=1, keepdims=True), axis=0, keepdims=True) / WIDTH_B
        ob_ref[0, t] = o * lax.rsqrt(ms + EPS) * gb_ref[...]


def _sample_attn_kernel(sink_ref, qa_ref, kn_ref, cak_ref, cav_ref, qb_ref, knb_ref, vnb_ref, ck16_ref, cv16_ref,
                        ck4_ref, cv4_ref, ga_ref, gb_ref, oa_ref, ob_ref):
    _sample_group_a(sink_ref, qa_ref, kn_ref, cak_ref, cav_ref, ga_ref, oa_ref)
    _sample_group_b(qb_ref, knb_ref, vnb_ref, ck16_ref, cv16_ref, ck4_ref, cv4_ref, gb_ref, ob_ref)


def sample_attention(proj_s, layer, sink, cache_a_k, cache_a_v, cache_b_k, cache_b_v, g_a, g_b):
    nseq = proj_s.shape[0] // T_DEC
    depth = cache_a_k.shape[0]
    p3 = proj_s.reshape(nseq, T_DEC, PROJ_WIDTH)
    qa = p3[:, :, :WIDTH_A].reshape(nseq, T_DEC, N_HEADS_A, HEAD_DIM).transpose(0, 2, 1, 3)
    qa = qa.reshape(nseq, N_HEADS_A * T_DEC, HEAD_DIM)
    qb = p3[:, :, COL_QB:COL_KB].reshape(nseq, T_DEC, N_HEADS_B, HEAD_DIM)
    knb = p3[:, :, COL_KB:COL_VB].reshape(nseq, T_DEC, N_HEADS_B, HEAD_DIM)
    vnb = p3[:, :, COL_VB:].reshape(nseq, T_DEC, N_HEADS_B, HEAD_DIM)
    ca_k = cache_a_k.reshape(depth, nseq, WINDOW_A, KV_WIDTH_A)
    ca_v = cache_a_v.reshape(depth, nseq, WINDOW_A, KV_WIDTH_A)
    cb16_k = cache_b_k.reshape(depth, nseq, ROWS_16, 16, N_HEADS_B, HEAD_DIM)
    cb16_v = cache_b_v.reshape(depth, nseq, ROWS_16, 16, N_HEADS_B, HEAD_DIM)
    last4 = WINDOW_B_MAX // ROWS_4 - 1
    a_spec = pl.BlockSpec((1, 1, WINDOW_A, KV_WIDTH_A), lambda b: (layer, b, 0, 0))
    b16_spec = pl.BlockSpec((1, 1, ROWS_16, T_DEC, N_HEADS_B, HEAD_DIM), lambda b: (layer, b, 0, 0, 0, 0))
    b4_spec = pl.BlockSpec((1, 1, ROWS_4, N_HEADS_B, HEAD_DIM), lambda b: (layer, b, last4, 0, 0))
    tok_spec = pl.BlockSpec((1, T_DEC, N_HEADS_B, HEAD_DIM), lambda b: (b, 0, 0, 0))
    oa, ob = pl.pallas_call(
        _sample_attn_kernel,
        grid=(nseq,),
        in_specs=[pl.BlockSpec(memory_space=pltpu.SMEM),
                  pl.BlockSpec((1, N_HEADS_A * T_DEC, HEAD_DIM), lambda b: (b, 0, 0)),
                  pl.BlockSpec((1, T_DEC, PROJ_WIDTH), lambda b: (b, 0, 0)),
                  a_spec, a_spec, tok_spec, tok_spec, tok_spec, b16_spec, b16_spec, b4_spec, b4_spec,
                  pl.BlockSpec((1, WIDTH_A), lambda b: (0, 0)),
                  pl.BlockSpec((N_HEADS_B, HEAD_DIM), lambda b: (0, 0))],
        out_specs=[pl.BlockSpec((1, T_DEC, WIDTH_A), lambda b: (b, 0, 0)), tok_spec],
        out_shape=[jax.ShapeDtypeStruct((nseq, T_DEC, WIDTH_A), F32),
                   jax.ShapeDtypeStruct((nseq, T_DEC, N_HEADS_B, HEAD_DIM), F32)],
        compiler_params=_params(("arbitrary",), 48),
        name="sample_attn",
    )(sink.reshape(1, N_HEADS_A), qa, p3, ca_k, ca_v, qb, knb, vnb, cb16_k, cb16_v, cache_b_k, cache_b_v,
      g_a.reshape(1, WIDTH_A), g_b.reshape(N_HEADS_B, HEAD_DIM))
    return jnp.concatenate([oa.reshape(nseq * T_DEC, WIDTH_A), ob.reshape(nseq * T_DEC, WIDTH_B)], axis=1)


CACHE_BLOCK_BYTES = 8 * MIB


def _cache_shift_kernel(cache_ref, new_ref, out_ref):
    rows, t_new = cache_ref.shape[2], new_ref.shape[2]
    out_ref[:, :, :rows - t_new] = cache_ref[:, :, t_new:]
    out_ref[:, :, rows - t_new:] = new_ref[...]


def cache_shift(cache, new_rows):
    depth, n_seq, rows, heads, hd = cache.shape
    t_new = new_rows.shape[2]
    seq_bytes = rows * max(heads, 8) * hd * cache.dtype.itemsize
    nb = max(1, min(n_seq, CACHE_BLOCK_BYTES // seq_bytes))
    assert n_seq % nb == 0
    return pl.pallas_call(
        _cache_shift_kernel,
        grid=(depth, n_seq // nb),
        in_specs=[pl.BlockSpec((1, nb, rows, heads, hd), lambda l, b: (l, b, 0, 0, 0)),
                  pl.BlockSpec((1, nb, t_new, heads, hd), lambda l, b: (l, b, 0, 0, 0))],
        out_specs=pl.BlockSpec((1, nb, rows, heads, hd), lambda l, b: (l, b, 0, 0, 0)),
        out_shape=jax.ShapeDtypeStruct(cache.shape, cache.dtype),
        compiler_params=_params(("arbitrary", "arbitrary"), 48),
        name="cache_shift",
    )(cache, new_rows)


def _cache_tail_kernel(buf_ref, new_ref, out_ref):
    out_ref[...] = new_ref[...]


def cache_tail(buf, new_rows):
    depth, n_seq, rows, heads, hd = buf.shape
    t_new = new_rows.shape[2]
    assert rows % t_new == 0
    return pl.pallas_call(
        _cache_tail_kernel,
        grid=(depth,),
        in_specs=[pl.BlockSpec(memory_space=pl.ANY),
                  pl.BlockSpec((1, n_seq, t_new, heads, hd), lambda l: (l, 0, 0, 0, 0))],
        out_specs=pl.BlockSpec((1, n_seq, t_new, heads, hd), lambda l: (l, 0, rows // t_new - 1, 0, 0)),
        out_shape=jax.ShapeDtypeStruct(buf.shape, buf.dtype),
        input_output_aliases={0: 0},
        compiler_params=_params(("arbitrary",), 16),
        name="cache_tail",
    )(buf, new_rows)


def _group_starts(te_ref, t):
    return (t == 0) | (te_ref[t] != te_ref[jnp.maximum(t - 1, 0)])


def _gate_up_body(te_ref, tr_ref, x_ref, wg_ref, wu_ref, a_ref, wgb_ref, wub_ref):
    t = pl.program_id(1)
    n_rows = tr_ref[t]
    half = x_ref.shape[0] // 2

    @pl.when((n_rows > 0) & _group_starts(te_ref, t))
    def _():
        wgb_ref[...] = wg_ref[...].astype(BF16)
        wub_ref[...] = wu_ref[...].astype(BF16)

    def compute(rows):
        x = x_ref[rows, :]
        g = jnp.dot(x, wgb_ref[...], preferred_element_type=F32)
        u = jnp.dot(x, wub_ref[...], preferred_element_type=F32)
        a_ref[rows, :] = (g * jax.nn.sigmoid(g) * u).astype(BF16)

    @pl.when(n_rows > half)
    def _():
        compute(slice(None))

    @pl.when((n_rows > 0) & (n_rows <= half))
    def _():
        compute(slice(0, half))

    @pl.when(n_rows <= half)
    def _():
        a_ref[half:, :] = jnp.zeros((half, a_ref.shape[1]), BF16)

    @pl.when(n_rows == 0)
    def _():
        a_ref[:half, :] = jnp.zeros((half, a_ref.shape[1]), BF16)


def _gate_up_kernel(te_ref, tr_ref, x_ref, wg_ref, wu_ref, a_ref, wgb_ref, wub_ref):
    _gate_up_body(te_ref, tr_ref, x_ref, wg_ref, wu_ref, a_ref, wgb_ref, wub_ref)


def _gate_up_relay_kernel(te_ref, tr_ref, tab_ref, x_ref, wg_ref, wu_ref, *rest, n_chunks, shift):
    ck_ref, cv_ref = rest[0], rest[1]
    a_ref, ok_ref, ov_ref, wgb_ref, wub_ref, buf_ref, sem_in, sem_out = rest[-8:]
    s = pl.program_id(0) * pl.num_programs(1) + pl.program_id(1)
    r_chunk = buf_ref.shape[2]

    def reads(j, slot):
        l, b, r0 = tab_ref[3 * j], tab_ref[3 * j + 1], tab_ref[3 * j + 2]
        return [pltpu.make_async_copy(src.at[l, b, pl.ds(r0 + shift, r_chunk)], buf_ref.at[slot, n], sem_in.at[slot, n])
                for n, src in enumerate((ck_ref, cv_ref))]

    def writes(j, slot):
        l, b, r0 = tab_ref[3 * j], tab_ref[3 * j + 1], tab_ref[3 * j + 2]
        return [pltpu.make_async_copy(buf_ref.at[slot, n], dst.at[l, b, pl.ds(r0, r_chunk)], sem_out.at[slot, n])
                for n, dst in enumerate((ok_ref, ov_ref))]

    slot = s % 2

    @pl.when(s == 0)
    def _():
        for cp in reads(0, 0):
            cp.start(priority=1)

    @pl.when(s < n_chunks)
    def _():
        for cp in reads(s, slot):
            cp.wait()
        for cp in writes(s, slot):
            cp.start(priority=1)

    @pl.when((s >= 1) & (s <= n_chunks))
    def _():
        for cp in writes(s - 1, 1 - slot):
            cp.wait()

    @pl.when(s + 1 < n_chunks)
    def _():
        for cp in reads(s + 1, 1 - slot):
            cp.start(priority=1)

    _gate_up_body(te_ref, tr_ref, x_ref, wg_ref, wu_ref, a_ref, wgb_ref, wub_ref)


def _down_kernel(te_ref, tr_ref, a_ref, wd_ref, *rest):
    r_ref = rest[0] if len(rest) == 3 else None
    o_ref, wdb_ref = rest[-2:]
    t = pl.program_id(1)
    n_rows = tr_ref[t]
    half = a_ref.shape[0] // 2

    @pl.when((n_rows > 0) & _group_starts(te_ref, t))
    def _():
        wdb_ref[...] = wd_ref[...].astype(BF16)

    def compute(rows):
        y = jnp.dot(a_ref[rows, :], wdb_ref[...], preferred_element_type=F32)
        o_ref[rows, :] = y if r_ref is None else r_ref[rows, :] + y

    def skip(rows):
        o_ref[rows, :] = jnp.zeros((half, o_ref.shape[1]), F32) if r_ref is None else r_ref[rows, :]

    @pl.when(n_rows > half)
    def _():
        compute(slice(None))

    @pl.when((n_rows > 0) & (n_rows <= half))
    def _():
        compute(slice(0, half))

    @pl.when(n_rows <= half)
    def _():
        skip(slice(half, 2 * half))

    @pl.when(n_rows == 0)
    def _():
        skip(slice(0, half))


def relay_table(layers, n_seq, rows, r_chunk):
    per_seq = -(-rows // r_chunk)
    assert n_seq >= 2
    tab = [(l, b, min(q * r_chunk, rows - r_chunk)) for l in layers for q in range(per_seq) for b in range(n_seq)]
    return np.asarray(tab, np.int32).reshape(-1)


def grouped_swiglu(x, tile_group, tile_rows, w_gate, w_up, w_down, idx, tm, tf, tn, residual=None, relay=None):
    p, d = x.shape
    d_ff = w_gate.shape[3]
    n_tiles = p // tm
    x_spec = pl.BlockSpec((tm, d), lambda f, t, *_: (t, 0))
    w_spec = pl.BlockSpec((None, None, d, tf), lambda f, t, te, *_: (idx, te[t], 0, f))
    a_spec = pl.BlockSpec((tm, tf), lambda f, t, *_: (t, f))
    a_sds = jax.ShapeDtypeStruct((p, d_ff), BF16)
    w_scratch = [pltpu.VMEM((d, tf), BF16), pltpu.VMEM((d, tf), BF16)]
    grid = (d_ff // tf, n_tiles)
    if relay is None:
        act = pl.pallas_call(
            _gate_up_kernel,
            grid_spec=pltpu.PrefetchScalarGridSpec(
                num_scalar_prefetch=2, grid=grid, in_specs=[x_spec, w_spec, w_spec], out_specs=a_spec,
                scratch_shapes=w_scratch),
            out_shape=a_sds,
            compiler_params=_params(("arbitrary", "arbitrary"), 60),
            name="swiglu_gate_up",
        )(tile_group, tile_rows, x, w_gate, w_up)
        filled = None
    else:
        cache_k, cache_v, layers, out_k, out_v, r_chunk = relay
        _, n_seq, rows, heads, hd = cache_k.shape
        tab = relay_table(layers, n_seq, rows - T_DEC, r_chunk)
        n_chunks = tab.shape[0] // 3
        assert n_chunks < grid[0] * grid[1]
        any_spec = pl.BlockSpec(memory_space=pl.ANY)
        prev = [] if out_k is None else [out_k, out_v]
        n_in = 3 + 3 + 2
        kern = functools.partial(_gate_up_relay_kernel, n_chunks=n_chunks, shift=T_DEC)
        act, ok, ov = pl.pallas_call(
            kern,
            grid_spec=pltpu.PrefetchScalarGridSpec(
                num_scalar_prefetch=3, grid=grid,
                in_specs=[x_spec, w_spec, w_spec, any_spec, any_spec] + [any_spec] * len(prev),
                out_specs=[a_spec, any_spec, any_spec],
                scratch_shapes=w_scratch + [pltpu.VMEM((2, 2, r_chunk, heads, hd), cache_k.dtype),
                                            pltpu.SemaphoreType.DMA((2, 2)), pltpu.SemaphoreType.DMA((2, 2))]),
            out_shape=[a_sds, jax.ShapeDtypeStruct(cache_k.shape, cache_k.dtype),
                       jax.ShapeDtypeStruct(cache_v.shape, cache_v.dtype)],
            input_output_aliases={n_in: 1, n_in + 1: 2} if prev else {},
            compiler_params=_params(("arbitrary", "arbitrary"), 60),
            name="swiglu_gate_up_relay",
        )(tile_group, tile_rows, jnp.asarray(tab), x, w_gate, w_up, cache_k, cache_v, *prev)
        filled = (ok, ov)
    in_specs = [pl.BlockSpec((tm, d_ff), lambda n, t, *_: (t, 0)),
                pl.BlockSpec((None, None, d_ff, tn), lambda n, t, te, *_: (idx, te[t], 0, n))]
    args = [act, w_down]
    if residual is not None:
        in_specs.append(pl.BlockSpec((tm, tn), lambda n, t, *_: (t, n)))
        args.append(residual)
    out = pl.pallas_call(
        _down_kernel,
        grid_spec=pltpu.PrefetchScalarGridSpec(
            num_scalar_prefetch=2,
            grid=(d // tn, n_tiles),
            in_specs=in_specs,
            out_specs=pl.BlockSpec((tm, tn), lambda n, t, *_: (t, n)),
            scratch_shapes=[pltpu.VMEM((d_ff, tn), BF16)]),
        out_shape=jax.ShapeDtypeStruct((p, d), F32),
        compiler_params=_params(("arbitrary", "arbitrary"), 60),
        name="swiglu_down",
    )(tile_group, tile_rows, *args)
    return out if relay is None else (out, filled)


def dense_ffn(h, x, w_gate, w_up, w_down, idx, tm, tf, tn):
    n_tiles = h.shape[0] // tm
    one_group = jnp.zeros((n_tiles,), jnp.int32)
    full_tiles = jnp.full((n_tiles,), tm, jnp.int32)
    return grouped_swiglu(h, one_group, full_tiles, w_gate[:, None], w_up[:, None], w_down[:, None], idx,
                          tm, tf, tn, residual=x)


ROUTE_LANES = 128
LANE_CHUNKS = D_MODEL // 128


def _router_kernel(x_ref, g_ref, wr_ref, h_ref, route_ref):
    h = _rms(x_ref[...], g_ref[...])
    for c in range(LANE_CHUNKS):
        h_ref[pl.ds(c, h.shape[0], stride=LANE_CHUNKS), :] = h[:, c * HEAD_DIM:(c + 1) * HEAD_DIM]
    logits = jnp.dot(h, wr_ref[...], preferred_element_type=F32, precision=lax.Precision.HIGHEST)
    lane = lax.broadcasted_iota(jnp.int32, logits.shape, 1).astype(F32)
    lg = jnp.where(lane < N_EXPERTS, logits, -jnp.inf)
    m1 = jnp.max(lg, axis=-1, keepdims=True)
    i1 = jnp.min(jnp.where(lg == m1, lane, float(ROUTE_LANES)), axis=-1, keepdims=True)
    lg2 = jnp.where(lane == i1, -jnp.inf, lg)
    m2 = jnp.max(lg2, axis=-1, keepdims=True)
    i2 = jnp.min(jnp.where(lg2 == m2, lane, float(ROUTE_LANES)), axis=-1, keepdims=True)
    e = jnp.exp(m2 - m1)
    g1 = 1.0 / (1.0 + e)
    g2 = e / (1.0 + e)
    route_ref[...] = jnp.where(lane == 0, i1, jnp.where(lane == 1, i2,
                                                        jnp.where(lane == 2, g1, jnp.where(lane == 3, g2, 0.0))))


def router(x, g, w_router, tm):
    t, d = x.shape
    wr = jnp.pad(w_router, ((0, 0), (0, ROUTE_LANES - N_EXPERTS)))
    h, route = pl.pallas_call(
        _router_kernel,
        grid=(t // tm,),
        in_specs=[pl.BlockSpec((tm, d), lambda i: (i, 0)),
                  pl.BlockSpec((1, d), lambda i: (0, 0)),
                  pl.BlockSpec((d, ROUTE_LANES), lambda i: (0, 0))],
        out_specs=[pl.BlockSpec((tm * LANE_CHUNKS, 128), lambda i: (i, 0)),
                   pl.BlockSpec((tm, ROUTE_LANES), lambda i: (i, 0))],
        out_shape=[jax.ShapeDtypeStruct((t * LANE_CHUNKS, 128), F32), jax.ShapeDtypeStruct((t, ROUTE_LANES), F32)],
        compiler_params=_params(("arbitrary",), 40),
        name="router",
    )(x, g.reshape(1, d), wr)
    top_i = route[:, :TOP_K].astype(jnp.int32)
    gates = route[:, TOP_K:2 * TOP_K]
    return h, top_i, gates


def _row_copy(src_ref, dst_ref, src_row, dst_row, sem):
    return pltpu.make_async_copy(src_ref.at[pl.ds(src_row, 1)], dst_ref.at[pl.ds(dst_row, 1)], sem)


def _gather_kernel(idx_ref, src_ref, o_ref, buf_ref, sem):
    rows = o_ref.shape[0]
    base = pl.program_id(0) * rows

    def token_copy(r):
        src_row = pl.multiple_of(idx_ref[base + r] * LANE_CHUNKS, LANE_CHUNKS)
        dst_row = pl.multiple_of(r * LANE_CHUNKS, LANE_CHUNKS)
        return pltpu.make_async_copy(src_ref.at[pl.ds(src_row, LANE_CHUNKS)],
                                     buf_ref.at[pl.ds(dst_row, LANE_CHUNKS)], sem)

    def start(pair, carry):
        for k in range(2):
            token_copy(2 * pair + k).start(priority=k)
        return carry

    def wait(r, carry):
        token_copy(r).wait()
        return carry

    lax.fori_loop(0, rows // 2, start, 0)
    lax.fori_loop(0, rows, wait, 0)
    for c in range(LANE_CHUNKS):
        o_ref[:, c * 128:(c + 1) * 128] = buf_ref[pl.ds(c, rows, stride=LANE_CHUNKS), :].astype(o_ref.dtype)


def gather_rows(src, idx, rows_per_step, out_dtype):
    n = idx.shape[0]
    d = LANE_CHUNKS * src.shape[1]
    return pl.pallas_call(
        _gather_kernel,
        grid_spec=pltpu.PrefetchScalarGridSpec(
            num_scalar_prefetch=1,
            grid=(n // rows_per_step,),
            in_specs=[pl.BlockSpec(memory_space=pl.ANY)],
            out_specs=pl.BlockSpec((rows_per_step, d), lambda i, idx_ref: (i, 0)),
            scratch_shapes=[pltpu.VMEM((rows_per_step * LANE_CHUNKS, src.shape[1]), src.dtype),
                            pltpu.SemaphoreType.DMA(())]),
        out_shape=jax.ShapeDtypeStruct((n, d), out_dtype),
        compiler_params=_params(("arbitrary",), 32),
        name="gather_rows",
    )(idx, src)


def _combine_kernel(pos_ref, x_ref, gate_ref, ys_ref, o_ref, buf_ref, sem):
    rows = x_ref.shape[0]
    base = pl.program_id(0) * rows

    def copies(r):
        return [_row_copy(ys_ref, buf_ref.at[k], pos_ref[TOP_K * (base + r) + k], r, sem) for k in range(TOP_K)]

    def start(r, carry):
        for k, cp in enumerate(copies(r)):
            cp.start(priority=k)
        return carry

    def wait(r, carry):
        for cp in copies(r):
            cp.wait()
        return carry

    lax.fori_loop(0, rows, start, 0)
    lax.fori_loop(0, rows, wait, 0)
    gates = gate_ref[...]
    y = x_ref[...]
    for k in range(TOP_K):
        y = y + gates[:, k:k + 1] * buf_ref[k]
    o_ref[...] = y


def combine_rows(x, gates, ys, pos, rows_per_step):
    t, d = x.shape
    return pl.pallas_call(
        _combine_kernel,
        grid_spec=pltpu.PrefetchScalarGridSpec(
            num_scalar_prefetch=1,
            grid=(t // rows_per_step,),
            in_specs=[pl.BlockSpec((rows_per_step, d), lambda i, pos_ref: (i, 0)),
                      pl.BlockSpec((rows_per_step, TOP_K), lambda i, pos_ref: (i, 0)),
                      pl.BlockSpec(memory_space=pl.ANY)],
            out_specs=pl.BlockSpec((rows_per_step, d), lambda i, pos_ref: (i, 0)),
            scratch_shapes=[pltpu.VMEM((TOP_K, rows_per_step, d), F32), pltpu.SemaphoreType.DMA(())]),
        out_shape=jax.ShapeDtypeStruct((t, d), F32),
        compiler_params=_params(("arbitrary",), 32),
        name="combine_rows",
    )(pos, x, gates, ys)


def moe_ffn(x, g, w_router, w_gate, w_up, w_down, idx, tm, tf, tn, relay):
    t = x.shape[0]
    h, top_i, gates = router(x, g, w_router, 640)
    ids = top_i.reshape(-1)
    onehot = (ids[:, None] == jnp.arange(N_EXPERTS)[None, :]).astype(jnp.int32)
    counts = jnp.sum(onehot, axis=0)
    rank = jnp.take_along_axis(jnp.cumsum(onehot, axis=0) - onehot, ids[:, None], axis=1)[:, 0]
    tiles = (counts + tm - 1) // tm
    tile_end = jnp.cumsum(tiles)
    group_start = (tile_end - tiles) * tm
    pos = group_start[ids] + rank
    n_tiles = (TOP_K * t) // tm + N_EXPERTS
    slot_row = jnp.zeros((n_tiles * tm,), jnp.int32).at[pos].set(jnp.arange(TOP_K * t, dtype=jnp.int32) // TOP_K)
    tile_idx = jnp.arange(n_tiles, dtype=jnp.int32)
    tile_live = tile_idx < tile_end[-1]
    tile_expert = jnp.minimum(jnp.sum((tile_idx[:, None] >= tile_end[None, :]).astype(jnp.int32), axis=1),
                              N_EXPERTS - 1)
    first_tile = (tile_end - tiles)[tile_expert]
    tile_rows = jnp.where(tile_live, jnp.clip(counts[tile_expert] - (tile_idx - first_tile) * tm, 0, tm), 0)
    last_live = jnp.take(tile_expert, jnp.maximum(tile_end[-1] - 1, 0))
    tile_expert = jnp.where(tile_live, tile_expert, last_live).astype(jnp.int32)
    xs = gather_rows(h, slot_row, tm, BF16)
    ys, filled = grouped_swiglu(xs, tile_expert, tile_rows.astype(jnp.int32), w_gate, w_up, w_down, idx,
                                tm, tf, tn, relay=relay)
    return combine_rows(x, gates, ys, pos.astype(jnp.int32), 128), filled


MOE_TILE = 384
RELAY_ROWS = 409

def kernel(x_prompt, x_sample, cache_a_k, cache_a_v, cache_b_k, cache_b_v, g_mix, w_in, sink_a, g_out_a, g_out_b,
           w_out, g_ffn, w_gate_dense, w_up_dense, w_down_dense, w_router, w_gate_moe, w_up_moe, w_down_moe,
           g_final):
    n_batch, seq, d = x_prompt.shape
    n_seq, t_dec, _ = x_sample.shape
    depth = w_in.shape[0]
    tp = n_batch * seq
    ts = n_seq * t_dec
    t_all = tp + ts
    tm = 640
    assert t_all % tm == 0 and t_dec == T_DEC and tp % BLOCK == 0 and ts == BLOCK
    assert cache_a_k.shape[2] == WINDOW_A and cache_b_k.shape[2] == WINDOW_B_MAX
    keep_a, keep_b = min(WINDOW_A, seq), min(WINDOW_B_MAX, seq)
    x = jnp.concatenate([x_prompt.reshape(tp, d), x_sample.reshape(ts, d)], axis=0)
    prompt_kv = [[] for _ in range(4)]
    sample_kv = [[] for _ in range(4)]
    shifted_b = (None, None)
    for l in range(depth):
        h = rmsnorm_rows(x, g_mix[l], BF16, tm, 0, t_all // tm)
        proj = matmul_layer(h, w_in, l, tm, 1536)
        proj_s = proj[tp:]
        oa = band_attention_a(proj, sink_a[l], n_batch, seq)
        obs, lses = zip(*[band_attention_b(proj, n_batch, seq, r) for _, r in DILATED_BRANCHES])
        o_p = merge_prompt(oa, obs, lses, g_out_a[l], g_out_b[l], 256)
        o_s = sample_attention(proj_s, l, sink_a[l], cache_a_k, cache_a_v, cache_b_k, cache_b_v,
                               g_out_a[l], g_out_b[l])
        o_all = jnp.concatenate([o_p, o_s.astype(BF16)], axis=0)
        x = matmul_layer(o_all, w_out, l, tm, 1024, residual=x)
        if l % 2 == 0:
            h2 = rmsnorm_rows(x, g_ffn[l], BF16, tm, 0, t_all // tm)
            x = dense_ffn(h2, x, w_gate_dense, w_up_dense, w_down_dense, l // 2, tm, 512, 512)
        else:
            moe_layers = range(1, depth, 2)
            n = moe_layers.index(l)
            share = range(n * depth // len(moe_layers), (n + 1) * depth // len(moe_layers))
            x, shifted_b = moe_ffn(x, g_ffn[l], w_router[l // 2], w_gate_moe, w_up_moe, w_down_moe, l // 2,
                                   MOE_TILE, 1024, 512,
                                   relay=(cache_b_k, cache_b_v, share, *shifted_b, RELAY_ROWS))
        for n, (c0, heads, keep) in enumerate(((COL_KA, KV_HEADS_A, keep_a), (COL_VA, KV_HEADS_A, keep_a),
                                               (COL_KB, N_HEADS_B, keep_b), (COL_VB, N_HEADS_B, keep_b))):
            c1 = c0 + heads * HEAD_DIM
            prompt_kv[n] += [proj[(b + 1) * seq - keep:(b + 1) * seq, c0:c1].reshape(keep, heads, HEAD_DIM)
                             for b in range(n_batch)]
            sample_kv[n].append(proj_s[:, c0:c1].reshape(n_seq, t_dec, heads, HEAD_DIM))
    y_prompt = rmsnorm_rows(x, g_final, F32, 512, 0, tp // 512).reshape(n_batch, seq, d)
    y_sample = rmsnorm_rows(x, g_final, F32, ts, tp // ts, 1).reshape(n_seq, t_dec, d)
    new_sample = [cache_shift(c, jnp.stack(rows)) for c, rows in zip((cache_a_k, cache_a_v), sample_kv[:2])]
    new_sample += [cache_tail(buf, jnp.stack(rows)) for buf, rows in zip(shifted_b, sample_kv[2:])]
    new_prompt = [jnp.stack(p).reshape(depth, n_batch, *p[0].shape) for p in prompt_kv]
    return (y_prompt, y_sample) + tuple(new_prompt) + tuple(new_sample)
```

```python
import functools

import jax
import jax.numpy as jnp
import numpy as np
from jax import lax
from jax.experimental import pallas as pl
from jax.experimental.pallas import tpu as pltpu

F32 = jnp.float32
BF16 = jnp.bfloat16

D_MODEL = 2048
HEAD_DIM = 128
N_HEADS_A = 8
KV_HEADS_A = 2
GROUP_A = 4
N_HEADS_B = 8
WIDTH_A = N_HEADS_A * HEAD_DIM
KV_WIDTH_A = KV_HEADS_A * HEAD_DIM
WIDTH_B = N_HEADS_B * HEAD_DIM
PROJ_WIDTH = WIDTH_A + 2 * KV_WIDTH_A + 3 * WIDTH_B
COL_KA = WIDTH_A
COL_VA = COL_KA + KV_WIDTH_A
COL_QB = COL_VA + KV_WIDTH_A
COL_KB = COL_QB + WIDTH_B
COL_VB = COL_KB + WIDTH_B
WINDOW_A = 128
DILATED_BRANCHES = ((128, 1), (512, 4), (2048, 16))
WINDOW_B_MAX = 2048
BLOCK = 128
N_EXPERTS = 8
TOP_K = 2
EPS = 1e-5
NEG_INF = -1e30
SCALE = HEAD_DIM ** -0.5
SLOPES = tuple(2.0 ** (-8.0 * i / (N_HEADS_A + N_HEADS_B)) for i in range(1, N_HEADS_A + N_HEADS_B + 1))
SLOPES_A = SLOPES[:N_HEADS_A]
SLOPES_B = SLOPES[N_HEADS_A:]

MIB = 1024 * 1024


def _params(semantics, vmem_mib):
    return pltpu.CompilerParams(dimension_semantics=semantics, vmem_limit_bytes=vmem_mib * MIB)


def _bdot(a, b):
    return jnp.dot(a.astype(BF16), b.astype(BF16), preferred_element_type=F32)


def _bdot_nt(a, b):
    return lax.dot_general(a.astype(BF16), b.astype(BF16), (((1,), (1,)), ((), ())),
                           preferred_element_type=F32)


def _rms(x, g):
    return x * lax.rsqrt(jnp.mean(x * x, axis=-1, keepdims=True) + EPS) * g


def _rmsnorm_kernel(x_ref, g_ref, o_ref):
    o_ref[...] = _rms(x_ref[...], g_ref[...]).astype(o_ref.dtype)


def rmsnorm_rows(x, g, out_dtype, row_block, first_block, n_blocks):
    d = x.shape[1]
    return pl.pallas_call(
        _rmsnorm_kernel,
        grid=(n_blocks,),
        in_specs=[pl.BlockSpec((row_block, d), lambda i: (i + first_block, 0)),
                  pl.BlockSpec((1, d), lambda i: (0, 0))],
        out_specs=pl.BlockSpec((row_block, d), lambda i: (i, 0)),
        out_shape=jax.ShapeDtypeStruct((n_blocks * row_block, d), out_dtype),
        compiler_params=_params(("arbitrary",), 40),
        name="rmsnorm",
    )(x, g.reshape(1, d))


def _matmul_kernel(x_ref, w_ref, o_ref, wb_ref):
    @pl.when(pl.program_id(1) == 0)
    def _():
        wb_ref[...] = w_ref[...].astype(BF16)

    o_ref[...] = jnp.dot(x_ref[...], wb_ref[...], preferred_element_type=F32)


def _matmul_res_kernel(x_ref, w_ref, r_ref, o_ref, wb_ref):
    @pl.when(pl.program_id(1) == 0)
    def _():
        wb_ref[...] = w_ref[...].astype(BF16)

    o_ref[...] = r_ref[...] + jnp.dot(x_ref[...], wb_ref[...], preferred_element_type=F32)


def matmul_layer(x, w_all, layer, tm, tn, residual=None):
    t, k = x.shape
    n = w_all.shape[2]
    grid = (n // tn, t // tm)
    in_specs = [pl.BlockSpec((tm, k), lambda j, i: (i, 0)),
                pl.BlockSpec((None, k, tn), lambda j, i: (layer, 0, j))]
    args = [x, w_all]
    kern = _matmul_kernel
    if residual is not None:
        in_specs.append(pl.BlockSpec((tm, tn), lambda j, i: (i, j)))
        args.append(residual)
        kern = _matmul_res_kernel
    return pl.pallas_call(
        kern,
        grid=grid,
        in_specs=in_specs,
        out_specs=pl.BlockSpec((tm, tn), lambda j, i: (i, j)),
        out_shape=jax.ShapeDtypeStruct((t, n), F32),
        scratch_shapes=[pltpu.VMEM((k, tn), BF16)],
        compiler_params=_params(("arbitrary", "arbitrary"), 56),
        name="matmul",
    )(*args)


def _band_mask(has_prev):
    i = lax.broadcasted_iota(jnp.int32, (BLOCK, 2 * BLOCK), 0)
    j = lax.broadcasted_iota(jnp.int32, (BLOCK, 2 * BLOCK), 1)
    dist = BLOCK + i - j
    valid = (dist >= 0) & (dist <= BLOCK) & ((j >= BLOCK) | has_prev)
    return dist.astype(F32), valid


def _band_head(q, kp, kc, vp, vc, slope, dist_f, valid, sink):
    k = jnp.concatenate([kp, kc], axis=0)
    v = jnp.concatenate([vp, vc], axis=0)
    s = _bdot_nt(q, k) * SCALE - slope * dist_f
    s = jnp.where(valid, s, NEG_INF)
    m = jnp.max(s, axis=-1, keepdims=True)
    if sink is not None:
        m = jnp.maximum(m, sink)
    p = jnp.exp(s - m)
    l = jnp.sum(p, axis=-1, keepdims=True)
    if sink is not None:
        l = l + jnp.exp(sink - m)
    return _bdot(p, v) / l, m + jnp.log(l)


def _band_kernel(*refs, dilation, sub_chunks, heads, kv_group, has_sink, with_lse):
    refs = list(refs)
    sink_ref = refs.pop(0) if has_sink else None
    slope_ref, q_ref, kp_ref, kc_ref, vp_ref, vc_ref, o_ref = refs[:7]
    lse_ref = refs[7] if with_lse else None
    hg = pl.program_id(1)
    dist_f, valid_first = _band_mask(pl.program_id(2) > 0)
    _, valid_rest = _band_mask(True)
    span = BLOCK * dilation
    if dilation == STAGED_DILATION:
        _band_staged(refs[-1], slope_ref, q_ref, kp_ref, kc_ref, vp_ref, vc_ref, o_ref, lse_ref, hg,
                     dist_f, valid_first)
        return

    def rows_of(start, res):
        return pl.ds(start + res, BLOCK, stride=dilation) if dilation > 1 else pl.ds(start, BLOCK)

    for sc in range(sub_chunks):
        for res in range(dilation):
            rows = rows_of(sc * span, res)
            for hh in range(heads):
                cols = slice(hh * HEAD_DIM, (hh + 1) * HEAD_DIM)
                kcols = slice((hh // kv_group) * HEAD_DIM, (hh // kv_group + 1) * HEAD_DIM)
                if sc == 0:
                    kp, vp = kp_ref[rows_of(0, res), kcols], vp_ref[rows_of(0, res), kcols]
                else:
                    prev = rows_of((sc - 1) * span, res)
                    kp, vp = kc_ref[prev, kcols], vc_ref[prev, kcols]
                head = hg * heads + hh
                o, lse = _band_head(q_ref[rows, cols], kp, kc_ref[rows, kcols], vp, vc_ref[rows, kcols],
                                    slope_ref[0, head], dist_f, valid_first if sc == 0 else valid_rest,
                                    sink_ref[0, head] if has_sink else None)
                o_ref[rows, cols] = o
                if with_lse:
                    lse_ref[rows, cols] = jnp.broadcast_to(lse, (BLOCK, HEAD_DIM))


STAGED_DILATION = 16
STAGE_STRIDE = 4


def _band_staged(stage_ref, slope_ref, q_ref, kp_ref, kc_ref, vp_ref, vc_ref, o_ref, lse_ref, hg, dist_f, valid):
    n_stage = BLOCK * STAGED_DILATION // STAGE_STRIDE
    for c in range(STAGE_STRIDE):
        outer = pl.ds(c, n_stage, stride=STAGE_STRIDE)
        for n, ref in enumerate((q_ref, kp_ref, kc_ref, vp_ref, vc_ref)):
            stage_ref[n] = ref[outer, :]
        for a in range(STAGED_DILATION // STAGE_STRIDE):
            inner = pl.ds(a, BLOCK, stride=STAGE_STRIDE)
            o, lse = _band_head(stage_ref[0, inner, :], stage_ref[1, inner, :], stage_ref[2, inner, :],
                                stage_ref[3, inner, :], stage_ref[4, inner, :], slope_ref[0, hg], dist_f, valid, None)
            stage_ref[5, inner, :] = o
            stage_ref[6, inner, :] = jnp.broadcast_to(lse, (BLOCK, HEAD_DIM))
        o_ref[outer, :] = stage_ref[5]
        lse_ref[outer, :] = stage_ref[6]


def band_attention(proj, n_batch, seq, *, dilation, sub_chunks, heads, kv_heads, n_heads, q_col, k_col, v_col,
                   slopes, sink=None, with_lse=False, name):
    r = dilation
    span = BLOCK * r
    chunk = span * sub_chunks
    nc = seq // chunk
    w, kvw = heads * HEAD_DIM, kv_heads * HEAD_DIM
    assert seq % chunk == 0 and q_col % w == 0 and k_col % kvw == 0 and v_col % kvw == 0

    def row(b, hg, ic):
        return b * nc + ic

    def prev(b, hg, ic):
        return b * nc * sub_chunks + jnp.maximum(ic * sub_chunks - 1, 0)

    kern = functools.partial(_band_kernel, dilation=r, sub_chunks=sub_chunks, heads=heads,
                             kv_group=heads // kv_heads, has_sink=sink is not None, with_lse=with_lse)
    smem = pl.BlockSpec(memory_space=pltpu.SMEM)
    in_specs = [smem,
                pl.BlockSpec((chunk, w), lambda b, hg, ic: (row(b, hg, ic), q_col // w + hg)),
                pl.BlockSpec((span, kvw), lambda b, hg, ic: (prev(b, hg, ic), k_col // kvw + hg)),
                pl.BlockSpec((chunk, kvw), lambda b, hg, ic: (row(b, hg, ic), k_col // kvw + hg)),
                pl.BlockSpec((span, kvw), lambda b, hg, ic: (prev(b, hg, ic), v_col // kvw + hg)),
                pl.BlockSpec((chunk, kvw), lambda b, hg, ic: (row(b, hg, ic), v_col // kvw + hg))]
    args = [jnp.asarray([slopes], F32), proj, proj, proj, proj, proj]
    if sink is not None:
        in_specs.insert(0, smem)
        args.insert(0, sink.reshape(1, n_heads))
    out_sds = jax.ShapeDtypeStruct((n_batch * seq, n_heads * HEAD_DIM), F32)
    out_spec = pl.BlockSpec((chunk, w), lambda b, hg, ic: (row(b, hg, ic), hg))
    return pl.pallas_call(
        kern,
        grid=(n_batch, n_heads // heads, nc),
        in_specs=in_specs,
        out_specs=[out_spec, out_spec] if with_lse else out_spec,
        out_shape=[out_sds, out_sds] if with_lse else out_sds,
        scratch_shapes=([pltpu.VMEM((7, span // STAGE_STRIDE, HEAD_DIM), F32)]
                        if r == STAGED_DILATION else []),
        compiler_params=_params(("arbitrary",) * 3, 40),
        name=name,
    )(*args)


def band_attention_a(proj, sink, n_batch, seq):
    return band_attention(proj, n_batch, seq, dilation=1, sub_chunks=2, heads=N_HEADS_A, kv_heads=KV_HEADS_A,
                          n_heads=N_HEADS_A, q_col=0, k_col=COL_KA, v_col=COL_VA, slopes=SLOPES_A, sink=sink,
                          name="band_attn_a")


BRANCH_TILING = {1: (4, 4), 4: (4, 1), 16: (1, 1)}


def band_attention_b(proj, n_batch, seq, dilation):
    sub_chunks, heads = BRANCH_TILING[dilation]
    return band_attention(proj, n_batch, seq, dilation=dilation, sub_chunks=sub_chunks, heads=heads,
                          kv_heads=heads, n_heads=N_HEADS_B, q_col=COL_QB, k_col=COL_KB, v_col=COL_VB,
                          slopes=[s * dilation for s in SLOPES_B], with_lse=True, name="band_attn_b")


def _merge_kernel(oa_ref, o1_ref, o2_ref, o3_ref, l1_ref, l2_ref, l3_ref, ga_ref, gb_ref, out_ref):
    l1, l2, l3 = l1_ref[...], l2_ref[...], l3_ref[...]
    m = jnp.maximum(jnp.maximum(l1, l2), l3)
    e1, e2, e3 = jnp.exp(l1 - m), jnp.exp(l2 - m), jnp.exp(l3 - m)
    ob = (e1 * o1_ref[...] + e2 * o2_ref[...] + e3 * o3_ref[...]) / (e1 + e2 + e3)
    out_ref[:, :WIDTH_A] = _rms(oa_ref[...], ga_ref[...]).astype(out_ref.dtype)
    out_ref[:, WIDTH_A:] = _rms(ob, gb_ref[...]).astype(out_ref.dtype)


def merge_prompt(oa, obs, lses, g_a, g_b, row_block):
    t = oa.shape[0]
    spec = pl.BlockSpec((row_block, WIDTH_A), lambda i: (i, 0))
    gspec = pl.BlockSpec((1, WIDTH_A), lambda i: (0, 0))
    return pl.pallas_call(
        _merge_kernel,
        grid=(t // row_block,),
        in_specs=[spec] * 7 + [gspec, gspec],
        out_specs=pl.BlockSpec((row_block, WIDTH_A + WIDTH_B), lambda i: (i, 0)),
        out_shape=jax.ShapeDtypeStruct((t, WIDTH_A + WIDTH_B), BF16),
        compiler_params=_params(("arbitrary",), 40),
        name="merge_prompt",
    )(oa, *obs, *lses, g_a.reshape(1, WIDTH_A), g_b.reshape(1, WIDTH_B))


T_DEC = 4
ROWS_16 = WINDOW_B_MAX // 16
ROWS_4 = 512


def _select_by_index(idx, values):
    out = values[-1]
    for n in range(len(values) - 2, -1, -1):
        out = jnp.where(idx == n, values[n], out)
    return out


def _sample_group_a(sink_ref, qa_ref, kn_ref, cak_ref, cav_ref, ga_ref, oa_ref):
    kn = kn_ref[0]
    rows_a = GROUP_A * T_DEC
    for hk in range(KV_HEADS_A):
        q = qa_ref[0, hk * rows_a:(hk + 1) * rows_a, :]
        row = lax.broadcasted_iota(jnp.int32, (rows_a, 1), 0)
        t_row, g_row = row & (T_DEC - 1), row >> 2
        slope = _select_by_index(g_row, [SLOPES_A[hk * GROUP_A + g] for g in range(GROUP_A)])
        sink = _select_by_index(g_row, [sink_ref[0, hk * GROUP_A + g] for g in range(GROUP_A)])
        kc = cak_ref[0, 0, :, hk * HEAD_DIM:(hk + 1) * HEAD_DIM]
        vc = cav_ref[0, 0, :, hk * HEAD_DIM:(hk + 1) * HEAD_DIM]
        c = lax.broadcasted_iota(jnp.int32, (rows_a, WINDOW_A), 1)
        dist = t_row + WINDOW_A - c
        s_c = _bdot_nt(q, kc) * SCALE - slope * dist.astype(F32)
        s_c = jnp.where(dist <= WINDOW_A, s_c, NEG_INF)
        news = []
        for tp in range(T_DEC):
            k_row = kn[tp:tp + 1, COL_KA + hk * HEAD_DIM:COL_KA + (hk + 1) * HEAD_DIM]
            v_row = kn[tp:tp + 1, COL_VA + hk * HEAD_DIM:COL_VA + (hk + 1) * HEAD_DIM]
            s = jnp.sum(q * k_row, axis=-1, keepdims=True) * SCALE - slope * (t_row - tp).astype(F32)
            news.append((jnp.where(t_row >= tp, s, NEG_INF), v_row))
        m = jnp.maximum(jnp.max(s_c, axis=-1, keepdims=True), sink)
        for s_n, _ in news:
            m = jnp.maximum(m, s_n)
        p_c = jnp.exp(s_c - m)
        l = jnp.sum(p_c, axis=-1, keepdims=True) + jnp.exp(sink - m)
        acc = _bdot(p_c, vc)
        for s_n, v_row in news:
            p_n = jnp.exp(s_n - m)
            l = l + p_n
            acc = acc + p_n * v_row
        o = acc / l
        for g in range(GROUP_A):
            col = (hk * GROUP_A + g) * HEAD_DIM
            oa_ref[0, :, col:col + HEAD_DIM] = o[g * T_DEC:(g + 1) * T_DEC, :]
    oa_ref[0] = _rms(oa_ref[0], ga_ref[...])


def _sample_group_b(qb_ref, knb_ref, vnb_ref, ck16_ref, cv16_ref, ck4_ref, cv4_ref, gb_ref, ob_ref):
    head = lax.broadcasted_iota(jnp.int32, (1, N_HEADS_B, 1), 1)
    slope = _select_by_index(head, list(SLOPES_B))
    i3 = lax.broadcasted_iota(jnp.int32, (ROWS_16, 1, 1), 0)
    i3f = i3.astype(F32)
    for t in range(T_DEC):
        q = qb_ref[0, t] * SCALE

        def scores(k3, dist3):
            return jnp.sum(k3 * q[None], axis=-1, keepdims=True) - slope * dist3

        s16 = scores(ck16_ref[0, 0, :, t], WINDOW_B_MAX - 16.0 * i3f)
        s4 = scores(ck4_ref[0, 0, pl.ds(t, ROWS_16, stride=4)], ROWS_4 - 4.0 * i3f)
        s1 = scores(ck4_ref[0, 0, ROWS_4 - BLOCK:ROWS_4], BLOCK + t - i3f)
        s1 = jnp.where(i3 >= t, s1, NEG_INF)
        news = []
        for tp in range(t + 1):
            s = jnp.sum(knb_ref[0, tp] * q, axis=-1, keepdims=True) - slope[0] * float(t - tp)
            news.append((s, 3.0 if tp == t else 1.0, vnb_ref[0, tp]))
        m = jnp.maximum(jnp.maximum(jnp.max(s16, axis=0), jnp.max(s4, axis=0)), jnp.max(s1, axis=0))
        for s_n, _, _ in news:
            m = jnp.maximum(m, s_n)
        p16, p4, p1 = jnp.exp(s16 - m[None]), jnp.exp(s4 - m[None]), jnp.exp(s1 - m[None])
        l = jnp.sum(p16, axis=0) + jnp.sum(p4, axis=0) + jnp.sum(p1, axis=0)
        acc = (jnp.sum(p16 * cv16_ref[0, 0, :, t], axis=0)
               + jnp.sum(p4 * cv4_ref[0, 0, pl.ds(t, ROWS_16, stride=4)], axis=0)
               + jnp.sum(p1 * cv4_ref[0, 0, ROWS_4 - BLOCK:ROWS_4], axis=0))
        for s_n, wgt, v_new in news:
            p_n = wgt * jnp.exp(s_n - m)
            l = l + p_n
            acc = acc + p_n * v_new
        o = acc / l
        ms = jnp.sum(jnp.sum(o * o, axis=1, keepdims=True), axis=0, keepdims=True) / WIDTH_B
        ob_ref[0, t] = o * lax.rsqrt(ms + EPS) * gb_ref[...]


def _sample_attn_kernel(sink_ref, qa_ref, kn_ref, cak_ref, cav_ref, qb_ref, knb_ref, vnb_ref, ck16_ref, cv16_ref,
                        ck4_ref, cv4_ref, ga_ref, gb_ref, oa_ref, ob_ref):
    _sample_group_a(sink_ref, qa_ref, kn_ref, cak_ref, cav_ref, ga_ref, oa_ref)
    _sample_group_b(qb_ref, knb_ref, vnb_ref, ck16_ref, cv16_ref, ck4_ref, cv4_ref, gb_ref, ob_ref)


def sample_attention(proj_s, layer, sink, cache_a_k, cache_a_v, cache_b_k, cache_b_v, g_a, g_b):
    nseq = proj_s.shape[0] // T_DEC
    depth = cache_a_k.shape[0]
    p3 = proj_s.reshape(nseq, T_DEC, PROJ_WIDTH)
    qa = p3[:, :, :WIDTH_A].reshape(nseq, T_DEC, N_HEADS_A, HEAD_DIM).transpose(0, 2, 1, 3)
    qa = qa.reshape(nseq, N_HEADS_A * T_DEC, HEAD_DIM)
    qb = p3[:, :, COL_QB:COL_KB].reshape(nseq, T_DEC, N_HEADS_B, HEAD_DIM)
    knb = p3[:, :, COL_KB:COL_VB].reshape(nseq, T_DEC, N_HEADS_B, HEAD_DIM)
    vnb = p3[:, :, COL_VB:].reshape(nseq, T_DEC, N_HEADS_B, HEAD_DIM)
    ca_k = cache_a_k.reshape(depth, nseq, WINDOW_A, KV_WIDTH_A)
    ca_v = cache_a_v.reshape(depth, nseq, WINDOW_A, KV_WIDTH_A)
    cb16_k = cache_b_k.reshape(depth, nseq, ROWS_16, 16, N_HEADS_B, HEAD_DIM)
    cb16_v = cache_b_v.reshape(depth, nseq, ROWS_16, 16, N_HEADS_B, HEAD_DIM)
    last4 = WINDOW_B_MAX // ROWS_4 - 1
    a_spec = pl.BlockSpec((1, 1, WINDOW_A, KV_WIDTH_A), lambda b: (layer, b, 0, 0))
    b16_spec = pl.BlockSpec((1, 1, ROWS_16, T_DEC, N_HEADS_B, HEAD_DIM), lambda b: (layer, b, 0, 0, 0, 0))
    b4_spec = pl.BlockSpec((1, 1, ROWS_4, N_HEADS_B, HEAD_DIM), lambda b: (layer, b, last4, 0, 0))
    tok_spec = pl.BlockSpec((1, T_DEC, N_HEADS_B, HEAD_DIM), lambda b: (b, 0, 0, 0))
    oa, ob = pl.pallas_call(
        _sample_attn_kernel,
        grid=(nseq,),
        in_specs=[pl.BlockSpec(memory_space=pltpu.SMEM),
                  pl.BlockSpec((1, N_HEADS_A * T_DEC, HEAD_DIM), lambda b: (b, 0, 0)),
                  pl.BlockSpec((1, T_DEC, PROJ_WIDTH), lambda b: (b, 0, 0)),
                  a_spec, a_spec, tok_spec, tok_spec, tok_spec, b16_spec, b16_spec, b4_spec, b4_spec,
                  pl.BlockSpec((1, WIDTH_A), lambda b: (0, 0)),
                  pl.BlockSpec((N_HEADS_B, HEAD_DIM), lambda b: (0, 0))],
        out_specs=[pl.BlockSpec((1, T_DEC, WIDTH_A), lambda b: (b, 0, 0)), tok_spec],
        out_shape=[jax.ShapeDtypeStruct((nseq, T_DEC, WIDTH_A), F32),
                   jax.ShapeDtypeStruct((nseq, T_DEC, N_HEADS_B, HEAD_DIM), F32)],
        compiler_params=_params(("arbitrary",), 48),
        name="sample_attn",
    )(sink.reshape(1, N_HEADS_A), qa, p3, ca_k, ca_v, qb, knb, vnb, cb16_k, cb16_v, cache_b_k, cache_b_v,
      g_a.reshape(1, WIDTH_A), g_b.reshape(N_HEADS_B, HEAD_DIM))
    return jnp.concatenate([oa.reshape(nseq * T_DEC, WIDTH_A), ob.reshape(nseq * T_DEC, WIDTH_B)], axis=1)


CACHE_BLOCK_BYTES = 8 * MIB


def _cache_shift_kernel(cache_ref, new_ref, out_ref):
    rows, t_new = cache_ref.shape[2], new_ref.shape[2]
    out_ref[:, :, :rows - t_new] = cache_ref[:, :, t_new:]
    out_ref[:, :, rows - t_new:] = new_ref[...]


def cache_shift(cache, new_rows):
    depth, n_seq, rows, heads, hd = cache.shape
    t_new = new_rows.shape[2]
    seq_bytes = rows * max(heads, 8) * hd * cache.dtype.itemsize
    nb = max(1, min(n_seq, CACHE_BLOCK_BYTES // seq_bytes))
    assert n_seq % nb == 0
    return pl.pallas_call(
        _cache_shift_kernel,
        grid=(depth, n_seq // nb),
        in_specs=[pl.BlockSpec((1, nb, rows, heads, hd), lambda l, b: (l, b, 0, 0, 0)),
                  pl.BlockSpec((1, nb, t_new, heads, hd), lambda l, b: (l, b, 0, 0, 0))],
        out_specs=pl.BlockSpec((1, nb, rows, heads, hd), lambda l, b: (l, b, 0, 0, 0)),
        out_shape=jax.ShapeDtypeStruct(cache.shape, cache.dtype),
        compiler_params=_params(("arbitrary", "arbitrary"), 48),
        name="cache_shift",
    )(cache, new_rows)


def _cache_tail_kernel(buf_ref, new_ref, out_ref):
    out_ref[...] = new_ref[...]


def cache_tail(buf, new_rows):
    depth, n_seq, rows, heads, hd = buf.shape
    t_new = new_rows.shape[2]
    assert rows % t_new == 0
    return pl.pallas_call(
        _cache_tail_kernel,
        grid=(depth,),
        in_specs=[pl.BlockSpec(memory_space=pl.ANY),
                  pl.BlockSpec((1, n_seq, t_new, heads, hd), lambda l: (l, 0, 0, 0, 0))],
        out_specs=pl.BlockSpec((1, n_seq, t_new, heads, hd), lambda l: (l, 0, rows // t_new - 1, 0, 0)),
        out_shape=jax.ShapeDtypeStruct(buf.shape, buf.dtype),
        input_output_aliases={0: 0},
        compiler_params=_params(("arbitrary",), 16),
        name="cache_tail",
    )(buf, new_rows)


def _group_starts(te_ref, t):
    return (t == 0) | (te_ref[t] != te_ref[jnp.maximum(t - 1, 0)])


def _gate_up_body(te_ref, tr_ref, x_ref, wg_ref, wu_ref, a_ref, wgb_ref, wub_ref):
    t = pl.program_id(1)
    n_rows = tr_ref[t]
    half = x_ref.shape[0] // 2

    @pl.when((n_rows > 0) & _group_starts(te_ref, t))
    def _():
        wgb_ref[...] = wg_ref[...].astype(BF16)
        wub_ref[...] = wu_ref[...].astype(BF16)

    def compute(rows):
        x = x_ref[rows, :]
        g = jnp.dot(x, wgb_ref[...], preferred_element_type=F32)
        u = jnp.dot(x, wub_ref[...], preferred_element_type=F32)
        a_ref[rows, :] = (g * jax.nn.sigmoid(g) * u).astype(BF16)

    @pl.when(n_rows > half)
    def _():
        compute(slice(None))

    @pl.when((n_rows > 0) & (n_rows <= half))
    def _():
        compute(slice(0, half))

    @pl.when(n_rows <= half)
    def _():
        a_ref[half:, :] = jnp.zeros((half, a_ref.shape[1]), BF16)

    @pl.when(n_rows == 0)
    def _():
        a_ref[:half, :] = jnp.zeros((half, a_ref.shape[1]), BF16)


def _gate_up_kernel(te_ref, tr_ref, x_ref, wg_ref, wu_ref, a_ref, wgb_ref, wub_ref):
    _gate_up_body(te_ref, tr_ref, x_ref, wg_ref, wu_ref, a_ref, wgb_ref, wub_ref)


def _gate_up_relay_kernel(te_ref, tr_ref, tab_ref, x_ref, wg_ref, wu_ref, *rest, n_chunks, shift):
    ck_ref, cv_ref = rest[0], rest[1]
    a_ref, ok_ref, ov_ref, wgb_ref, wub_ref, buf_ref, sem_in, sem_out = rest[-8:]
    s = pl.program_id(0) * pl.num_programs(1) + pl.program_id(1)
    r_chunk = buf_ref.shape[2]

    def reads(j, slot):
        l, b, r0 = tab_ref[3 * j], tab_ref[3 * j + 1], tab_ref[3 * j + 2]
        return [pltpu.make_async_copy(src.at[l, b, pl.ds(r0 + shift, r_chunk)], buf_ref.at[slot, n], sem_in.at[slot, n])
                for n, src in enumerate((ck_ref, cv_ref))]

    def writes(j, slot):
        l, b, r0 = tab_ref[3 * j], tab_ref[3 * j + 1], tab_ref[3 * j + 2]
        return [pltpu.make_async_copy(buf_ref.at[slot, n], dst.at[l, b, pl.ds(r0, r_chunk)], sem_out.at[slot, n])
                for n, dst in enumerate((ok_ref, ov_ref))]

    slot = s % 2

    @pl.when(s == 0)
    def _():
        for cp in reads(0, 0):
            cp.start()

    @pl.when(s < n_chunks)
    def _():
        for cp in reads(s, slot):
            cp.wait()
        for cp in writes(s, slot):
            cp.start()

    @pl.when((s >= 1) & (s <= n_chunks))
    def _():
        for cp in writes(s - 1, 1 - slot):
            cp.wait()

    @pl.when(s + 1 < n_chunks)
    def _():
        for cp in reads(s + 1, 1 - slot):
            cp.start()

    _gate_up_body(te_ref, tr_ref, x_ref, wg_ref, wu_ref, a_ref, wgb_ref, wub_ref)


def _down_kernel(te_ref, tr_ref, a_ref, wd_ref, *rest):
    r_ref = rest[0] if len(rest) == 3 else None
    o_ref, wdb_ref = rest[-2:]
    t = pl.program_id(1)
    n_rows = tr_ref[t]
    half = a_ref.shape[0] // 2

    @pl.when((n_rows > 0) & _group_starts(te_ref, t))
    def _():
        wdb_ref[...] = wd_ref[...].astype(BF16)

    def compute(rows):
        y = jnp.dot(a_ref[rows, :], wdb_ref[...], preferred_element_type=F32)
        o_ref[rows, :] = y if r_ref is None else r_ref[rows, :] + y

    def skip(rows):
        o_ref[rows, :] = jnp.zeros((half, o_ref.shape[1]), F32) if r_ref is None else r_ref[rows, :]

    @pl.when(n_rows > half)
    def _():
        compute(slice(None))

    @pl.when((n_rows > 0) & (n_rows <= half))
    def _():
        compute(slice(0, half))

    @pl.when(n_rows <= half)
    def _():
        skip(slice(half, 2 * half))

    @pl.when(n_rows == 0)
    def _():
        skip(slice(0, half))


def relay_table(layers, n_seq, rows, r_chunk):
    per_seq = -(-rows // r_chunk)
    assert n_seq >= 2
    tab = [(l, b, min(q * r_chunk, rows - r_chunk)) for l in layers for q in range(per_seq) for b in range(n_seq)]
    return np.asarray(tab, np.int32).reshape(-1)


def grouped_swiglu(x, tile_group, tile_rows, w_gate, w_up, w_down, idx, tm, tf, tn, residual=None, relay=None):
    p, d = x.shape
    d_ff = w_gate.shape[3]
    n_tiles = p // tm
    x_spec = pl.BlockSpec((tm, d), lambda f, t, *_: (t, 0))
    w_spec = pl.BlockSpec((None, None, d, tf), lambda f, t, te, *_: (idx, te[t], 0, f))
    a_spec = pl.BlockSpec((tm, tf), lambda f, t, *_: (t, f))
    a_sds = jax.ShapeDtypeStruct((p, d_ff), BF16)
    w_scratch = [pltpu.VMEM((d, tf), BF16), pltpu.VMEM((d, tf), BF16)]
    grid = (d_ff // tf, n_tiles)
    if relay is None:
        act = pl.pallas_call(
            _gate_up_kernel,
            grid_spec=pltpu.PrefetchScalarGridSpec(
                num_scalar_prefetch=2, grid=grid, in_specs=[x_spec, w_spec, w_spec], out_specs=a_spec,
                scratch_shapes=w_scratch),
            out_shape=a_sds,
            compiler_params=_params(("arbitrary", "arbitrary"), 60),
            name="swiglu_gate_up",
        )(tile_group, tile_rows, x, w_gate, w_up)
        filled = None
    else:
        cache_k, cache_v, layers, out_k, out_v, r_chunk = relay
        _, n_seq, rows, heads, hd = cache_k.shape
        tab = relay_table(layers, n_seq, rows - T_DEC, r_chunk)
        n_chunks = tab.shape[0] // 3
        assert n_chunks < grid[0] * grid[1]
        any_spec = pl.BlockSpec(memory_space=pl.ANY)
        prev = [] if out_k is None else [out_k, out_v]
        n_in = 3 + 3 + 2
        kern = functools.partial(_gate_up_relay_kernel, n_chunks=n_chunks, shift=T_DEC)
        act, ok, ov = pl.pallas_call(
            kern,
            grid_spec=pltpu.PrefetchScalarGridSpec(
                num_scalar_prefetch=3, grid=grid,
                in_specs=[x_spec, w_spec, w_spec, any_spec, any_spec] + [any_spec] * len(prev),
                out_specs=[a_spec, any_spec, any_spec],
                scratch_shapes=w_scratch + [pltpu.VMEM((2, 2, r_chunk, heads, hd), cache_k.dtype),
                                            pltpu.SemaphoreType.DMA((2, 2)), pltpu.SemaphoreType.DMA((2, 2))]),
            out_shape=[a_sds, jax.ShapeDtypeStruct(cache_k.shape, cache_k.dtype),
                       jax.ShapeDtypeStruct(cache_v.shape, cache_v.dtype)],
            input_output_aliases={n_in: 1, n_in + 1: 2} if prev else {},
            compiler_params=_params(("arbitrary", "arbitrary"), 60),
            name="swiglu_gate_up_relay",
        )(tile_group, tile_rows, jnp.asarray(tab), x, w_gate, w_up, cache_k, cache_v, *prev)
        filled = (ok, ov)
    in_specs = [pl.BlockSpec((tm, d_ff), lambda n, t, *_: (t, 0)),
                pl.BlockSpec((None, None, d_ff, tn), lambda n, t, te, *_: (idx, te[t], 0, n))]
    args = [act, w_down]
    if residual is not None:
        in_specs.append(pl.BlockSpec((tm, tn), lambda n, t, *_: (t, n)))
        args.append(residual)
    out = pl.pallas_call(
        _down_kernel,
        grid_spec=pltpu.PrefetchScalarGridSpec(
            num_scalar_prefetch=2,
            grid=(d // tn, n_tiles),
            in_specs=in_specs,
            out_specs=pl.BlockSpec((tm, tn), lambda n, t, *_: (t, n)),
            scratch_shapes=[pltpu.VMEM((d_ff, tn), BF16)]),
        out_shape=jax.ShapeDtypeStruct((p, d), F32),
        compiler_params=_params(("arbitrary", "arbitrary"), 60),
        name="swiglu_down",
    )(tile_group, tile_rows, *args)
    return out if relay is None else (out, filled)


def dense_ffn(h, x, w_gate, w_up, w_down, idx, tm, tf, tn):
    n_tiles = h.shape[0] // tm
    one_group = jnp.zeros((n_tiles,), jnp.int32)
    full_tiles = jnp.full((n_tiles,), tm, jnp.int32)
    return grouped_swiglu(h, one_group, full_tiles, w_gate[:, None], w_up[:, None], w_down[:, None], idx,
                          tm, tf, tn, residual=x)


ROUTE_LANES = 128
LANE_CHUNKS = D_MODEL // 128


def _router_kernel(x_ref, g_ref, wr_ref, h_ref, route_ref):
    h = _rms(x_ref[...], g_ref[...])
    for c in range(LANE_CHUNKS):
        h_ref[pl.ds(c, h.shape[0], stride=LANE_CHUNKS), :] = h[:, c * HEAD_DIM:(c + 1) * HEAD_DIM]
    logits = jnp.dot(h, wr_ref[...], preferred_element_type=F32, precision=lax.Precision.HIGHEST)
    lane = lax.broadcasted_iota(jnp.int32, logits.shape, 1).astype(F32)
    lg = jnp.where(lane < N_EXPERTS, logits, -jnp.inf)
    m1 = jnp.max(lg, axis=-1, keepdims=True)
    i1 = jnp.min(jnp.where(lg == m1, lane, float(ROUTE_LANES)), axis=-1, keepdims=True)
    lg2 = jnp.where(lane == i1, -jnp.inf, lg)
    m2 = jnp.max(lg2, axis=-1, keepdims=True)
    i2 = jnp.min(jnp.where(lg2 == m2, lane, float(ROUTE_LANES)), axis=-1, keepdims=True)
    e = jnp.exp(m2 - m1)
    g1 = 1.0 / (1.0 + e)
    g2 = e / (1.0 + e)
    route_ref[...] = jnp.where(lane == 0, i1, jnp.where(lane == 1, i2,
                                                        jnp.where(lane == 2, g1, jnp.where(lane == 3, g2, 0.0))))


def router(x, g, w_router, tm):
    t, d = x.shape
    wr = jnp.pad(w_router, ((0, 0), (0, ROUTE_LANES - N_EXPERTS)))
    h, route = pl.pallas_call(
        _router_kernel,
        grid=(t // tm,),
        in_specs=[pl.BlockSpec((tm, d), lambda i: (i, 0)),
                  pl.BlockSpec((1, d), lambda i: (0, 0)),
                  pl.BlockSpec((d, ROUTE_LANES), lambda i: (0, 0))],
        out_specs=[pl.BlockSpec((tm * LANE_CHUNKS, 128), lambda i: (i, 0)),
                   pl.BlockSpec((tm, ROUTE_LANES), lambda i: (i, 0))],
        out_shape=[jax.ShapeDtypeStruct((t * LANE_CHUNKS, 128), F32), jax.ShapeDtypeStruct((t, ROUTE_LANES), F32)],
        compiler_params=_params(("arbitrary",), 40),
        name="router",
    )(x, g.reshape(1, d), wr)
    top_i = route[:, :TOP_K].astype(jnp.int32)
    gates = route[:, TOP_K:2 * TOP_K]
    return h, top_i, gates


def _row_copy(src_ref, dst_ref, src_row, dst_row, sem):
    return pltpu.make_async_copy(src_ref.at[pl.ds(src_row, 1)], dst_ref.at[pl.ds(dst_row, 1)], sem)


def _gather_kernel(idx_ref, src_ref, o_ref, buf_ref, sem):
    rows = o_ref.shape[0]
    base = pl.program_id(0) * rows

    def token_copy(r):
        src_row = pl.multiple_of(idx_ref[base + r] * LANE_CHUNKS, LANE_CHUNKS)
        dst_row = pl.multiple_of(r * LANE_CHUNKS, LANE_CHUNKS)
        return pltpu.make_async_copy(src_ref.at[pl.ds(src_row, LANE_CHUNKS)],
                                     buf_ref.at[pl.ds(dst_row, LANE_CHUNKS)], sem)

    def start(pair, carry):
        for k in range(2):
            token_copy(2 * pair + k).start(priority=k)
        return carry

    def wait(r, carry):
        token_copy(r).wait()
        return carry

    lax.fori_loop(0, rows // 2, start, 0)
    lax.fori_loop(0, rows, wait, 0)
    for c in range(LANE_CHUNKS):
        o_ref[:, c * 128:(c + 1) * 128] = buf_ref[pl.ds(c, rows, stride=LANE_CHUNKS), :].astype(o_ref.dtype)


def gather_rows(src, idx, rows_per_step, out_dtype):
    n = idx.shape[0]
    d = LANE_CHUNKS * src.shape[1]
    return pl.pallas_call(
        _gather_kernel,
        grid_spec=pltpu.PrefetchScalarGridSpec(
            num_scalar_prefetch=1,
            grid=(n // rows_per_step,),
            in_specs=[pl.BlockSpec(memory_space=pl.ANY)],
            out_specs=pl.BlockSpec((rows_per_step, d), lambda i, idx_ref: (i, 0)),
            scratch_shapes=[pltpu.VMEM((rows_per_step * LANE_CHUNKS, src.shape[1]), src.dtype),
                            pltpu.SemaphoreType.DMA(())]),
        out_shape=jax.ShapeDtypeStruct((n, d), out_dtype),
        compiler_params=_params(("arbitrary",), 32),
        name="gather_rows",
    )(idx, src)


def _combine_kernel(pos_ref, x_ref, gate_ref, ys_ref, o_ref, buf_ref, sem):
    rows = x_ref.shape[0]
    base = pl.program_id(0) * rows

    def copies(r):
        return [_row_copy(ys_ref, buf_ref.at[k], pos_ref[TOP_K * (base + r) + k], r, sem) for k in range(TOP_K)]

    def start(r, carry):
        for k, cp in enumerate(copies(r)):
            cp.start(priority=k)
        return carry

    def wait(r, carry):
        for cp in copies(r):
            cp.wait()
        return carry

    lax.fori_loop(0, rows, start, 0)
    lax.fori_loop(0, rows, wait, 0)
    gates = gate_ref[...]
    y = x_ref[...]
    for k in range(TOP_K):
        y = y + gates[:, k:k + 1] * buf_ref[k]
    o_ref[...] = y


def combine_rows(x, gates, ys, pos, rows_per_step):
    t, d = x.shape
    return pl.pallas_call(
        _combine_kernel,
        grid_spec=pltpu.PrefetchScalarGridSpec(
            num_scalar_prefetch=1,
            grid=(t // rows_per_step,),
            in_specs=[pl.BlockSpec((rows_per_step, d), lambda i, pos_ref: (i, 0)),
                      pl.BlockSpec((rows_per_step, TOP_K), lambda i, pos_ref: (i, 0)),
                      pl.BlockSpec(memory_space=pl.ANY)],
            out_specs=pl.BlockSpec((rows_per_step, d), lambda i, pos_ref: (i, 0)),
            scratch_shapes=[pltpu.VMEM((TOP_K, rows_per_step, d), F32), pltpu.SemaphoreType.DMA(())]),
        out_shape=jax.ShapeDtypeStruct((t, d), F32),
        compiler_params=_params(("arbitrary",), 32),
        name="combine_rows",
    )(pos, x, gates, ys)


def moe_ffn(x, g, w_router, w_gate, w_up, w_down, idx, tm, tf, tn, relay):
    t = x.shape[0]
    h, top_i, gates = router(x, g, w_router, 640)
    ids = top_i.reshape(-1)
    onehot = (ids[:, None] == jnp.arange(N_EXPERTS)[None, :]).astype(jnp.int32)
    counts = jnp.sum(onehot, axis=0)
    rank = jnp.take_along_axis(jnp.cumsum(onehot, axis=0) - onehot, ids[:, None], axis=1)[:, 0]
    tiles = (counts + tm - 1) // tm
    tile_end = jnp.cumsum(tiles)
    group_start = (tile_end - tiles) * tm
    pos = group_start[ids] + rank
    n_tiles = (TOP_K * t) // tm + N_EXPERTS
    slot_row = jnp.zeros((n_tiles * tm,), jnp.int32).at[pos].set(jnp.arange(TOP_K * t, dtype=jnp.int32) // TOP_K)
    tile_idx = jnp.arange(n_tiles, dtype=jnp.int32)
    tile_live = tile_idx < tile_end[-1]
    tile_expert = jnp.minimum(jnp.sum((tile_idx[:, None] >= tile_end[None, :]).astype(jnp.int32), axis=1),
                              N_EXPERTS - 1)
    first_tile = (tile_end - tiles)[tile_expert]
    tile_rows = jnp.where(tile_live, jnp.clip(counts[tile_expert] - (tile_idx - first_tile) * tm, 0, tm), 0)
    last_live = jnp.take(tile_expert, jnp.maximum(tile_end[-1] - 1, 0))
    tile_expert = jnp.where(tile_live, tile_expert, last_live).astype(jnp.int32)
    xs = gather_rows(h, slot_row, tm, BF16)
    ys, filled = grouped_swiglu(xs, tile_expert, tile_rows.astype(jnp.int32), w_gate, w_up, w_down, idx,
                                tm, tf, tn, relay=relay)
    return combine_rows(x, gates, ys, pos.astype(jnp.int32), 128), filled


MOE_TILE = 384
RELAY_ROWS = 409

def kernel(x_prompt, x_sample, cache_a_k, cache_a_v, cache_b_k, cache_b_v, g_mix, w_in, sink_a, g_out_a, g_out_b,
           w_out, g_ffn, w_gate_dense, w_up_dense, w_down_dense, w_router, w_gate_moe, w_up_moe, w_down_moe,
           g_final):
    n_batch, seq, d = x_prompt.shape
    n_seq, t_dec, _ = x_sample.shape
    depth = w_in.shape[0]
    tp = n_batch * seq
    ts = n_seq * t_dec
    t_all = tp + ts
    tm = 640
    assert t_all % tm == 0 and t_dec == T_DEC and tp % BLOCK == 0 and ts == BLOCK
    assert cache_a_k.shape[2] == WINDOW_A and cache_b_k.shape[2] == WINDOW_B_MAX
    keep_a, keep_b = min(WINDOW_A, seq), min(WINDOW_B_MAX, seq)
    x = jnp.concatenate([x_prompt.reshape(tp, d), x_sample.reshape(ts, d)], axis=0)
    prompt_kv = [[] for _ in range(4)]
    sample_kv = [[] for _ in range(4)]
    shifted_b = (None, None)
    for l in range(depth):
        h = rmsnorm_rows(x, g_mix[l], BF16, tm, 0, t_all // tm)
        proj = matmul_layer(h, w_in, l, tm, 1536)
        proj_s = proj[tp:]
        oa = band_attention_a(proj, sink_a[l], n_batch, seq)
        obs, lses = zip(*[band_attention_b(proj, n_batch, seq, r) for _, r in DILATED_BRANCHES])
        o_p = merge_prompt(oa, obs, lses, g_out_a[l], g_out_b[l], 256)
        o_s = sample_attention(proj_s, l, sink_a[l], cache_a_k, cache_a_v, cache_b_k, cache_b_v,
                               g_out_a[l], g_out_b[l])
        o_all = jnp.concatenate([o_p, o_s.astype(BF16)], axis=0)
        x = matmul_layer(o_all, w_out, l, tm, 1024, residual=x)
        if l % 2 == 0:
            h2 = rmsnorm_rows(x, g_ffn[l], BF16, tm, 0, t_all // tm)
            x = dense_ffn(h2, x, w_gate_dense, w_up_dense, w_down_dense, l // 2, tm, 512, 512)
        else:
            moe_layers = range(1, depth, 2)
            n = moe_layers.index(l)
            share = range(n * depth // len(moe_layers), (n + 1) * depth // len(moe_layers))
            x, shifted_b = moe_ffn(x, g_ffn[l], w_router[l // 2], w_gate_moe, w_up_moe, w_down_moe, l // 2,
                                   MOE_TILE, 1024, 512,
                                   relay=(cache_b_k, cache_b_v, share, *shifted_b, RELAY_ROWS))
        for n, (c0, heads, keep) in enumerate(((COL_KA, KV_HEADS_A, keep_a), (COL_VA, KV_HEADS_A, keep_a),
                                               (COL_KB, N_HEADS_B, keep_b), (COL_VB, N_HEADS_B, keep_b))):
            c1 = c0 + heads * HEAD_DIM
            prompt_kv[n] += [proj[(b + 1) * seq - keep:(b + 1) * seq, c0:c1].reshape(keep, heads, HEAD_DIM)
                             for b in range(n_batch)]
            sample_kv[n].append(proj_s[:, c0:c1].reshape(n_seq, t_dec, heads, HEAD_DIM))
    y_prompt = rmsnorm_rows(x, g_final, F32, 512, 0, tp // 512).reshape(n_batch, seq, d)
    y_sample = rmsnorm_rows(x, g_final, F32, ts, tp // ts, 1).reshape(n_seq, t_dec, d)
    new_sample = [cache_shift(c, jnp.stack(rows)) for c, rows in zip((cache_a_k, cache_a_v), sample_kv[:2])]
    new_sample += [cache_tail(buf, jnp.stack(rows)) for buf, rows in zip(shifted_b, sample_kv[2:])]
    new_prompt = [jnp.stack(p).reshape(depth, n_batch, *p[0].shape) for p in prompt_kv]
    return (y_prompt, y_sample) + tuple(new_prompt) + tuple(new_sample)
```
